```python
import jax, jax.numpy as jnp
from jax import lax
import numpy as np

D_MODEL = 1024
BATCH = 4
SEQ = 4096
DEPTH = 4

D_MIX = D_MODEL
D_RG = D_MIX // 2
RG_BLOCKS = 8
RG_BLOCK_W = D_RG // RG_BLOCKS
CONV_W = 4
RG_C = 8.0
GLA_HEADS = 4
GLA_VD = D_MIX - D_RG
GLA_KD = GLA_VD // 2
GLA_DK = GLA_KD // GLA_HEADS
GLA_DV = GLA_VD // GLA_HEADS
GLA_RANK = 16
GLA_TAU = 16.0
GLA_CHUNK = 64
D_IN = 2 * D_RG + 2 * GLA_KD + 2 * GLA_VD + GLA_RANK
D_FF = 2816
N_EXPERTS = 8
TOP_K = 2
N_DENSE = (DEPTH + 1) // 2
N_MOE = DEPTH // 2
EPS = 1e-6

kernel_name = "hybrid_rglru_gla_moe_adaln_trunk"


def rms_norm(x, g):
    x32 = x.astype(jnp.float32)
    y = x32 * lax.rsqrt(jnp.mean(jnp.square(x32), axis=-1, keepdims=True) + EPS)
    return (y * g.astype(jnp.float32)).astype(x.dtype)


def modulate(h, shift, scale):
    return h * (1.0 + scale) + shift


def causal_depthwise_conv(u, w, b):
    s = u.shape[1]
    up = jnp.pad(u, ((0, 0), (CONV_W - 1, 0), (0, 0)))
    out = b
    for tap in range(CONV_W):
        out = out + up[:, tap:tap + s] * w[tap]
    return out


def _linear_combine(c1, c2):
    a1, b1 = c1
    a2, b2 = c2
    return a1 * a2, a2 * b1 + b2


def rg_lru(u, wa, ba, wx, bx, lam):
    bsz, s, d = u.shape
    ub = u.reshape(bsz, s, RG_BLOCKS, RG_BLOCK_W)
    r = jax.nn.sigmoid((jnp.einsum('bsnd,nde->bsne', ub, wa).reshape(bsz, s, d) + ba).astype(jnp.float32))
    i = jax.nn.sigmoid((jnp.einsum('bsnd,nde->bsne', ub, wx).reshape(bsz, s, d) + bx).astype(jnp.float32))
    log_a = -RG_C * r * jax.nn.softplus(-lam.astype(jnp.float32))
    a = jnp.exp(log_a)
    mult = jnp.sqrt(-jnp.expm1(2.0 * log_a))
    first = (jnp.arange(s) == 0)[None, :, None]
    mult = jnp.where(first, 1.0, mult)
    b_term = mult * i * u.astype(jnp.float32)
    _, h = lax.associative_scan(_linear_combine, (a, b_term), axis=1)
    return h.astype(u.dtype)


def gla_chunked(q, k, v, log_f):
    out_dtype = v.dtype
    bsz, s, nh, dk = q.shape
    dv = v.shape[-1]
    n = s // GLA_CHUNK
    q = q.astype(jnp.float32) * (dk ** -0.5)
    k = k.astype(jnp.float32)
    v = v.astype(jnp.float32)
    q = q.reshape(bsz, n, GLA_CHUNK, nh, dk)
    k = k.reshape(bsz, n, GLA_CHUNK, nh, dk)
    v = v.reshape(bsz, n, GLA_CHUNK, nh, dv)
    b = jnp.cumsum(log_f.astype(jnp.float32).reshape(bsz, n, GLA_CHUNK, nh, dk), axis=2)
    b_last = b[:, :, -1]
    b_ref = b[:, :, GLA_CHUNK // 2 - 1:GLA_CHUNK // 2]
    q_loc = q * jnp.exp(b - b_ref)
    k_loc = k * jnp.exp(b_ref - b)
    scores = jnp.einsum('bnihd,bnjhd->bnhij', q_loc, k_loc)
    causal = jnp.tril(jnp.ones((GLA_CHUNK, GLA_CHUNK), dtype=bool))
    scores = jnp.where(causal, scores, 0.0)
    o_intra = jnp.einsum('bnhij,bnjhv->bnihv', scores, v)
    k_end = k * jnp.exp(b_last[:, :, None] - b)
    u_chunk = jnp.einsum('bnjhd,bnjhv->bnhdv', k_end, v)
    decay = jnp.exp(b_last)

    def step(state, inp):
        dec, upd = inp
        return dec[..., None] * state + upd, state

    init = jnp.zeros((bsz, nh, dk, dv), jnp.float32)
    _, states = lax.scan(step, init, (jnp.moveaxis(decay, 1, 0), jnp.moveaxis(u_chunk, 1, 0)))
    states = jnp.moveaxis(states, 0, 1)
    o_inter = jnp.einsum('bnihd,bnhdv->bnihv', q * jnp.exp(b), states)
    o = (o_intra + o_inter).reshape(bsz, s, nh, dv)
    return o.astype(out_dtype)


def gla_branch(q, k, v, g, f_low, wg2, bg, norm_g):
    bsz, s, _ = q.shape
    log_f = jax.nn.log_sigmoid((f_low @ wg2 + bg).astype(jnp.float32)) / GLA_TAU
    o = gla_chunked(q.reshape(bsz, s, GLA_HEADS, GLA_DK),
                    k.reshape(bsz, s, GLA_HEADS, GLA_DK),
                    v.reshape(bsz, s, GLA_HEADS, GLA_DV),
                    log_f.reshape(bsz, s, GLA_HEADS, GLA_DK))
    o = rms_norm(o, norm_g)
    return o.reshape(bsz, s, GLA_VD) * jax.nn.silu(g)


def hybrid_mixer(h, w_in, conv_w, conv_b, wa, ba, wx, bx, lam, wg2, bg, gla_g, w_out):
    z = h @ w_in
    splits = np.cumsum([D_RG, D_RG, GLA_KD, GLA_KD, GLA_VD, GLA_VD]).tolist()
    rg_x, rg_gate, q, k, v, g, f_low = jnp.split(z, splits, axis=-1)
    u = causal_depthwise_conv(rg_x, conv_w, conv_b)
    rg_out = rg_lru(u, wa, ba, wx, bx, lam) * jax.nn.gelu(rg_gate)
    gla_out = gla_branch(q, k, v, g, f_low, wg2, bg, gla_g)
    return jnp.concatenate([rg_out, gla_out], axis=-1) @ w_out


def swiglu(h, w1, w3, w2):
    return (jax.nn.silu(h @ w1) * (h @ w3)) @ w2


def moe_swiglu(h, router_w, w1, w3, w2):
    logits = (h @ router_w).astype(jnp.float32)
    top_v, top_i = lax.top_k(logits, TOP_K)
    probs = jax.nn.softmax(top_v, axis=-1)
    gates = jnp.sum(jax.nn.one_hot(top_i, N_EXPERTS, dtype=jnp.float32) * probs[..., None], axis=-2)
    gates = gates.astype(h.dtype)
    y = jnp.zeros_like(h)
    for e in range(N_EXPERTS):
        y = y + gates[..., e:e + 1] * swiglu(h, w1[e], w3[e], w2[e])
    return y


def setup_inputs(seed: int = 0) -> dict:
    key = jax.random.key(seed)
    ks = jax.random.split(key, 32)

    def nrm(k, shape, scale):
        return scale * jax.random.normal(k, shape, jnp.float32)

    a0 = jax.random.uniform(ks[13], (DEPTH, D_RG), jnp.float32, 0.9, 0.999)
    a_root = a0 ** (1.0 / RG_C)
    rg_lambda = jnp.log(a_root) - jnp.log1p(-a_root)
    return {
        "x": nrm(ks[0], (BATCH, SEQ, D_MODEL), 1.0),
        "c": nrm(ks[1], (BATCH, D_MODEL), 1.0),
        "ada_w": nrm(ks[2], (DEPTH, D_MODEL, 6 * D_MODEL), 0.5 * D_MODEL ** -0.5),
        "ada_b": nrm(ks[3], (DEPTH, 6 * D_MODEL), 0.01),
        "norm_mix_g": 1.0 + nrm(ks[4], (DEPTH, D_MODEL), 0.05),
        "norm_ffn_g": 1.0 + nrm(ks[5], (DEPTH, D_MODEL), 0.05),
        "w_in": nrm(ks[6], (DEPTH, D_MODEL, D_IN), D_MODEL ** -0.5),
        "rg_conv_w": nrm(ks[7], (DEPTH, CONV_W, D_RG), CONV_W ** -0.5),
        "rg_conv_b": nrm(ks[8], (DEPTH, D_RG), 0.01),
        "rg_wa": nrm(ks[9], (DEPTH, RG_BLOCKS, RG_BLOCK_W, RG_BLOCK_W), RG_BLOCK_W ** -0.5),
        "rg_ba": nrm(ks[10], (DEPTH, D_RG), 0.01),
        "rg_wx": nrm(ks[11], (DEPTH, RG_BLOCKS, RG_BLOCK_W, RG_BLOCK_W), RG_BLOCK_W ** -0.5),
        "rg_bx": nrm(ks[12], (DEPTH, D_RG), 0.01),
        "rg_lambda": rg_lambda,
        "gla_wg2": nrm(ks[14], (DEPTH, GLA_RANK, GLA_KD), GLA_RANK ** -0.5),
        "gla_bg": nrm(ks[15], (DEPTH, GLA_KD), 0.1),
        "gla_norm_g": 1.0 + nrm(ks[16], (DEPTH, GLA_DV), 0.05),
        "w_out": nrm(ks[17], (DEPTH, D_MIX, D_MODEL), D_MIX ** -0.5),
        "ffn_w1": nrm(ks[18], (N_DENSE, D_MODEL, D_FF), D_MODEL ** -0.5),
        "ffn_w3": nrm(ks[19], (N_DENSE, D_MODEL, D_FF), D_MODEL ** -0.5),
        "ffn_w2": nrm(ks[20], (N_DENSE, D_FF, D_MODEL), D_FF ** -0.5),
        "router_w": nrm(ks[21], (N_MOE, D_MODEL, N_EXPERTS), D_MODEL ** -0.5),
        "moe_w1": nrm(ks[22], (N_MOE, N_EXPERTS, D_MODEL, D_FF), D_MODEL ** -0.5),
        "moe_w3": nrm(ks[23], (N_MOE, N_EXPERTS, D_MODEL, D_FF), D_MODEL ** -0.5),
        "moe_w2": nrm(ks[24], (N_MOE, N_EXPERTS, D_FF, D_MODEL), D_FF ** -0.5),
        "final_g": 1.0 + nrm(ks[25], (D_MODEL,), 0.05),
    }


def reference(x, c, ada_w, ada_b, norm_mix_g, norm_ffn_g, w_in, rg_conv_w, rg_conv_b,
              rg_wa, rg_ba, rg_wx, rg_bx, rg_lambda, gla_wg2, gla_bg, gla_norm_g, w_out,
              ffn_w1, ffn_w3, ffn_w2, router_w, moe_w1, moe_w3, moe_w2, final_g):
    c_act = jax.nn.silu(c)
    for layer in range(DEPTH):
        mod = c_act @ ada_w[layer] + ada_b[layer]
        sh_m, sc_m, gt_m, sh_f, sc_f, gt_f = [m[:, None, :] for m in jnp.split(mod, 6, axis=-1)]
        h = modulate(rms_norm(x, norm_mix_g[layer]), sh_m, sc_m)
        mix = hybrid_mixer(h, w_in[layer], rg_conv_w[layer], rg_conv_b[layer],
                           rg_wa[layer], rg_ba[layer], rg_wx[layer], rg_bx[layer], rg_lambda[layer],
                           gla_wg2[layer], gla_bg[layer], gla_norm_g[layer], w_out[layer])
        x = x + gt_m * mix
        h = modulate(rms_norm(x, norm_ffn_g[layer]), sh_f, sc_f)
        j = layer // 2
        if layer % 2 == 0:
            f = swiglu(h, ffn_w1[j], ffn_w3[j], ffn_w2[j])
        else:
            f = moe_swiglu(h, router_w[j], moe_w1[j], moe_w3[j], moe_w2[j])
        x = x + gt_f * f
    return rms_norm(x, final_g)
```

```python
import functools

import jax
import jax.numpy as jnp
from jax import lax
from jax.experimental import pallas as pl
from jax.experimental.pallas import tpu as pltpu

F32 = jnp.float32
BF16 = jnp.bfloat16

EPS = 1e-6
RG_C = 8.0
CONV_W = 4
RG_BLOCKS = 8
GLA_HEADS = 4
GLA_TAU = 16.0
GLA_CHUNK = 64
TOP_K = 2

LANES = 128
SUBLANES = 8
VMEM_LIMIT = 56 * 1024 * 1024

MIX_TS = 256
FFN_TM = 512
FFN_FC = 256


def _sigmoid(x):
    return 1.0 / (1.0 + jnp.exp(-x))


def _rms_mod(x, g, shift, scale):
    ms = jnp.mean(x * x, axis=-1, keepdims=True)
    y = x * lax.rsqrt(ms + EPS) * g
    return y * (1.0 + scale) + shift


def _log2(n):
    assert n > 0 and n & (n - 1) == 0, n
    return n.bit_length() - 1


def _idiv(x, n):
    return lax.shift_right_logical(x, _log2(n))


def _imod(x, n):
    _log2(n)
    return x & (n - 1)


def _shift_rows(x, d, fill, seg=None):
    rolled = pltpu.roll(x, d, axis=0)
    row = lax.broadcasted_iota(jnp.int32, x.shape, 0)
    if seg is not None:
        row = _imod(row, seg)
    return jnp.where(row >= d, rolled, fill)


def _mod_kernel(c_ref, w_ref, b_ref, o_ref):
    c = c_ref[...]
    ca = c * _sigmoid(c)
    o_ref[...] = jnp.dot(ca, w_ref[...], precision=lax.Precision.HIGHEST,
                         preferred_element_type=F32) + b_ref[...]


def _adaln_mod(c, ada_w, ada_b):
    depth, d, n = ada_w.shape
    bsz = c.shape[0]
    rows = -(-bsz // SUBLANES) * SUBLANES
    c_pad = jnp.pad(c, ((0, rows - bsz), (0, 0)))
    tn = 1536
    assert n % tn == 0
    out = pl.pallas_call(
        _mod_kernel,
        grid=(depth, n // tn),
        in_specs=[
            pl.BlockSpec((rows, d), lambda l, j: (0, 0)),
            pl.BlockSpec((None, d, tn), lambda l, j: (l, 0, j)),
            pl.BlockSpec((None, 1, tn), lambda l, j: (l, 0, j)),
        ],
        out_specs=pl.BlockSpec((None, rows, tn), lambda l, j: (l, 0, j)),
        out_shape=jax.ShapeDtypeStruct((depth, rows, n), F32),
        compiler_params=pltpu.CompilerParams(
            dimension_semantics=("arbitrary", "arbitrary"),
            vmem_limit_bytes=VMEM_LIMIT),
        name="adaln_mod",
    )(c_pad, ada_w, ada_b.reshape(depth, 1, n))
    return out[:, :bsz, :]


def _mixer_kernel(x_ref, mod_ref, ng_ref, win_ref, cw_ref, cb_ref, wg_ref, bgate_ref,
                  lam_ref, wg2_ref, bg_ref, gg_ref, wout_ref, o_ref,
                  tail_ref, hst_ref, gst_ref, *, d_rg, kd, vd):
    ts = x_ref.shape[0]
    dk = kd // GLA_HEADS
    dv = vd // GLA_HEADS
    s_idx = pl.program_id(1)

    @pl.when(s_idx == 0)
    def _():
        tail_ref[...] = jnp.zeros_like(tail_ref)
        hst_ref[...] = jnp.zeros_like(hst_ref)
        gst_ref[...] = jnp.zeros_like(gst_ref)

    x = x_ref[...]
    mod = mod_ref[...]
    h = _rms_mod(x, ng_ref[...], mod[0:1], mod[1:2]).astype(BF16)
    z = jnp.dot(h, win_ref[...], preferred_element_type=F32)

    o0 = 0
    rg_x = z[:, o0:o0 + d_rg]; o0 += d_rg
    rg_gate = z[:, o0:o0 + d_rg]; o0 += d_rg
    q = z[:, o0:o0 + kd]; o0 += kd
    k = z[:, o0:o0 + kd]; o0 += kd
    v = z[:, o0:o0 + vd]; o0 += vd
    g = z[:, o0:o0 + vd]; o0 += vd
    f_low = z[:, o0:o0 + LANES]

    cat = jnp.concatenate([tail_ref[...], rg_x], axis=0)
    cw = cw_ref[...]
    u = cb_ref[...] + cw[CONV_W - 1:CONV_W] * rg_x
    for tap in range(CONV_W - 1):
        sh = CONV_W - 1 - tap
        u = u + cw[tap:tap + 1] * pltpu.roll(cat, sh, axis=0)[SUBLANES:SUBLANES + ts]
    tail_ref[...] = rg_x[ts - SUBLANES:ts]

    gates = jnp.dot(u.astype(BF16), wg_ref[...], preferred_element_type=F32) + bgate_ref[...]
    r = _sigmoid(gates[:, :d_rg])
    i_gate = _sigmoid(gates[:, d_rg:])
    lam = lam_ref[...]
    sp = jnp.maximum(-lam, 0.0) + jnp.log1p(jnp.exp(-jnp.abs(lam)))
    log_a = -RG_C * r * sp
    a = jnp.exp(log_a)
    mult = jnp.sqrt(-jnp.tanh(log_a) * (a * a + 1.0))
    row = lax.broadcasted_iota(jnp.int32, (ts, d_rg), 0)
    mult = jnp.where((row == 0) & (s_idx == 0), 1.0, mult)
    b_term = mult * i_gate * u
    d = 1
    while d < ts:
        b_term = a * _shift_rows(b_term, d, 0.0) + b_term
        a = a * _shift_rows(a, d, 1.0)
        d *= 2
    h_rg = a * hst_ref[0:1] + b_term
    hst_ref[...] = jnp.broadcast_to(h_rg[ts - 1:ts], hst_ref.shape)
    gelu = 0.5 * rg_gate * (1.0 + jnp.tanh(0.7978845608028654 * (rg_gate + 0.044715 * rg_gate ** 3)))
    rg_out = h_rg * gelu

    fz = jnp.dot(f_low.astype(BF16), wg2_ref[...], preferred_element_type=F32) + bg_ref[...]
    log_f = (jnp.minimum(fz, 0.0) - jnp.log1p(jnp.exp(-jnp.abs(fz)))) * (1.0 / GLA_TAU)
    bcum = log_f
    d = 1
    while d < GLA_CHUNK:
        bcum = bcum + _shift_rows(bcum, d, 0.0, seg=GLA_CHUNK)
        d *= 2

    scale = dk ** -0.5
    nh = GLA_HEADS
    c_len = GLA_CHUNK
    r_i = lax.broadcasted_iota(jnp.int32, (nh * c_len, kd), 0)
    c_i = lax.broadcasted_iota(jnp.int32, (nh * c_len, kd), 1)
    head_mask = _idiv(r_i, c_len) == _idiv(c_i, dk)
    r_j = lax.broadcasted_iota(jnp.int32, (nh * c_len, c_len), 0)
    c_j = lax.broadcasted_iota(jnp.int32, (nh * c_len, c_len), 1)
    causal = c_j <= _imod(r_j, c_len)
    r_s = lax.broadcasted_iota(jnp.int32, (vd, kd), 0)
    c_s = lax.broadcasted_iota(jnp.int32, (vd, kd), 1)
    bd_mask = _idiv(r_s, dv) == _idiv(c_s, dk)

    state = gst_ref[...]
    o_chunks = []
    for ci in range(ts // c_len):
        lo = ci * c_len
        bc = bcum[lo:lo + c_len]
        qc = q[lo:lo + c_len] * scale
        kc = k[lo:lo + c_len]
        vc = v[lo:lo + c_len].astype(BF16)
        b_ref_row = bc[c_len // 2 - 1:c_len // 2]
        b_last = bc[c_len - 1:c_len]
        q_loc = (qc * jnp.exp(bc - b_ref_row)).astype(BF16)
        k_loc = (kc * jnp.exp(b_ref_row - bc)).astype(BF16)
        q_stack = jnp.where(head_mask, jnp.concatenate([q_loc] * nh, axis=0), jnp.zeros((), BF16))
        scores = lax.dot_general(q_stack, k_loc, (((1,), (1,)), ((), ())),
                                 preferred_element_type=F32)
        p = jnp.where(causal, scores, 0.0).astype(BF16)
        oi = jnp.dot(p, vc, preferred_element_type=F32)
        o_intra = jnp.concatenate(
            [oi[hh * c_len:(hh + 1) * c_len, hh * dv:(hh + 1) * dv] for hh in range(nh)], axis=1)
        q_b = (qc * jnp.exp(bc)).astype(BF16)
        o_inter = lax.dot_general(q_b, state.astype(BF16), (((1,), (1,)), ((), ())),
                                  preferred_element_type=F32)
        o_chunks.append(o_intra + o_inter)
        k_end = (kc * jnp.exp(b_last - bc)).astype(BF16)
        upd = lax.dot_general(vc, k_end, (((0,), (0,)), ((), ())),
                              preferred_element_type=F32)
        state = jnp.exp(b_last) * state + jnp.where(bd_mask, upd, 0.0)
    gst_ref[...] = state
    o = jnp.concatenate(o_chunks, axis=0)

    gg = gg_ref[...]
    heads = []
    for hh in range(nh):
        oh = o[:, hh * dv:(hh + 1) * dv]
        ms = jnp.mean(oh * oh, axis=-1, keepdims=True)
        heads.append(oh * lax.rsqrt(ms + EPS) * gg)
    gla_out = jnp.concatenate(heads, axis=1) * (g * _sigmoid(g))

    mix_in = jnp.concatenate([rg_out, gla_out], axis=1).astype(BF16)
    mix = jnp.dot(mix_in, wout_ref[...], preferred_element_type=F32)
    o_ref[...] = x + mod[2:3] * mix


def _const_spec(shape):
    nd = len(shape)
    return pl.BlockSpec(shape, lambda *_: (0,) * nd, pipeline_mode=pl.Buffered(1))


def _mixer(x, mod6, ng, win_p, cw, cb, wgate, bgate, lam, wg2_p, bg, gg, wout):
    bsz, s, d = x.shape
    d_rg = cw.shape[1]
    kd = bg.shape[1]
    vd = wout.shape[0] - d_rg
    ts = MIX_TS
    assert s % ts == 0 and ts % GLA_CHUNK == 0
    kern = functools.partial(_mixer_kernel, d_rg=d_rg, kd=kd, vd=vd)
    consts = [ng, win_p, cw, cb, wgate, bgate, lam, wg2_p, bg, gg, wout]
    return pl.pallas_call(
        kern,
        grid=(bsz, s // ts),
        in_specs=[
            pl.BlockSpec((None, ts, d), lambda b, i: (b, i, 0)),
            pl.BlockSpec((None, 6, d), lambda b, i: (b, 0, 0)),
        ] + [_const_spec(a.shape) for a in consts],
        out_specs=pl.BlockSpec((None, ts, d), lambda b, i: (b, i, 0)),
        out_shape=jax.ShapeDtypeStruct(x.shape, F32),
        scratch_shapes=[
            pltpu.VMEM((SUBLANES, d_rg), F32),
            pltpu.VMEM((SUBLANES, d_rg), F32),
            pltpu.VMEM((vd, kd), F32),
        ],
        compiler_params=pltpu.CompilerParams(
            dimension_semantics=("arbitrary", "arbitrary"),
            vmem_limit_bytes=VMEM_LIMIT),
        name="mixer",
    )(x, mod6, *consts)


def _swiglu_acc(h, w1_ref, w3_ref, w2_ref, acc):
    f = w1_ref.shape[-1]
    for j in range(f // FFN_FC):
        sl = slice(j * FFN_FC, (j + 1) * FFN_FC)
        a = jnp.dot(h, w1_ref[:, sl], preferred_element_type=F32)
        b = jnp.dot(h, w3_ref[:, sl], preferred_element_type=F32)
        p = (a * _sigmoid(a) * b).astype(BF16)
        acc = acc + jnp.dot(p, w2_ref[sl, :], preferred_element_type=F32)
    return acc


def _ffn_kernel(x_ref, mod_ref, ng_ref, w1_ref, w3_ref, w2_ref, o_ref):
    x = x_ref[...]
    mod = mod_ref[...]
    h = _rms_mod(x, ng_ref[...], mod[3:4], mod[4:5]).astype(BF16)
    acc = _swiglu_acc(h, w1_ref, w3_ref, w2_ref, jnp.zeros(x.shape, F32))
    o_ref[...] = x + mod[5:6] * acc


def _ffn(x, mod6, ng, w1, w3, w2):
    bsz, s, d = x.shape
    f = w1.shape[1]
    tm = FFN_TM
    assert s % tm == 0 and f % FFN_FC == 0
    return pl.pallas_call(
        _ffn_kernel,
        grid=(bsz, s // tm),
        in_specs=[
            pl.BlockSpec((None, tm, d), lambda b, i: (b, i, 0)),
            pl.BlockSpec((None, 6, d), lambda b, i: (b, 0, 0)),
            _const_spec(ng.shape), _const_spec(w1.shape), _const_spec(w3.shape),
            _const_spec(w2.shape),
        ],
        out_specs=pl.BlockSpec((None, tm, d), lambda b, i: (b, i, 0)),
        out_shape=jax.ShapeDtypeStruct(x.shape, F32),
        compiler_params=pltpu.CompilerParams(
            dimension_semantics=("arbitrary", "arbitrary"),
            vmem_limit_bytes=VMEM_LIMIT),
        name="ffn_dense",
    )(x, mod6, ng, w1, w3, w2)


def _top2(logits, n_exp):
    lane = lax.broadcasted_iota(jnp.int32, logits.shape, 1).astype(F32)
    neg = jnp.float32(-jnp.inf)
    lg = jnp.where(lane < n_exp, logits, neg)
    m1 = jnp.max(lg, axis=-1, keepdims=True)
    i1 = jnp.min(jnp.where(lg == m1, lane, float(LANES)), axis=-1, keepdims=True)
    lg2 = jnp.where(lane == i1, neg, lg)
    m2 = jnp.max(lg2, axis=-1, keepdims=True)
    i2 = jnp.min(jnp.where(lg2 == m2, lane, float(LANES)), axis=-1, keepdims=True)
    ex = jnp.exp(m2 - m1)
    return i1, i2, 1.0 / (1.0 + ex), ex / (1.0 + ex)


def _moe_kernel(x_ref, mod_ref, ng_ref, rw_ref, w1_ref, w3_ref, w2_ref, o_ref,
                h_ref, gate_ref, acc_ref, *, n_exp):
    e = pl.program_id(2)

    @pl.when(e == 0)
    def _():
        x = x_ref[...]
        mod = mod_ref[...]
        h = _rms_mod(x, ng_ref[...], mod[3:4], mod[4:5])
        h_ref[...] = h.astype(BF16)
        logits = jnp.dot(h, rw_ref[...], precision=lax.Precision.HIGHEST,
                         preferred_element_type=F32)
        i1, i2, p1, p2 = _top2(logits, n_exp)
        lane = lax.broadcasted_iota(jnp.int32, logits.shape, 1).astype(F32)
        gate_ref[...] = jnp.where(lane == i1, p1, 0.0) + jnp.where(lane == i2, p2, 0.0)
        acc_ref[...] = jnp.zeros_like(acc_ref)

    lane = lax.broadcasted_iota(jnp.int32, gate_ref.shape, 1)
    gate_col = jnp.sum(jnp.where(lane == e, gate_ref[...], 0.0), axis=-1, keepdims=True)
    y = _swiglu_acc(h_ref[...], w1_ref, w3_ref, w2_ref, jnp.zeros(acc_ref.shape, F32))
    acc_ref[...] += gate_col * y

    @pl.when(e == n_exp - 1)
    def _():
        o_ref[...] = x_ref[...] + mod_ref[5:6, :] * acc_ref[...]


def _moe(x, mod6, ng, rw_p, w1, w3, w2):
    bsz, s, d = x.shape
    n_exp, _, f = w1.shape
    tm = FFN_TM
    assert s % tm == 0 and f % FFN_FC == 0
    kern = functools.partial(_moe_kernel, n_exp=n_exp)
    return pl.pallas_call(
        kern,
        grid=(bsz, s // tm, n_exp),
        in_specs=[
            pl.BlockSpec((None, tm, d), lambda b, i, e: (b, i, 0)),
            pl.BlockSpec((None, 6, d), lambda b, i, e: (b, 0, 0)),
            _const_spec(ng.shape), _const_spec(rw_p.shape),
            pl.BlockSpec((None, d, f), lambda b, i, e: (e, 0, 0)),
            pl.BlockSpec((None, d, f), lambda b, i, e: (e, 0, 0)),
            pl.BlockSpec((None, f, d), lambda b, i, e: (e, 0, 0)),
        ],
        out_specs=pl.BlockSpec((None, tm, d), lambda b, i, e: (b, i, 0)),
        out_shape=jax.ShapeDtypeStruct(x.shape, F32),
        scratch_shapes=[
            pltpu.VMEM((tm, d), BF16),
            pltpu.VMEM((tm, LANES), F32),
            pltpu.VMEM((tm, d), F32),
        ],
        compiler_params=pltpu.CompilerParams(
            dimension_semantics=("arbitrary", "arbitrary", "arbitrary"),
            vmem_limit_bytes=VMEM_LIMIT),
        name="moe_dense",
    )(x, mod6, ng, rw_p, w1, w3, w2)


def _final_norm_kernel(x_ref, g_ref, o_ref):
    x = x_ref[...]
    ms = jnp.mean(x * x, axis=-1, keepdims=True)
    o_ref[...] = x * lax.rsqrt(ms + EPS) * g_ref[...]


def _final_norm(x, g):
    bsz, s, d = x.shape
    tm = FFN_TM
    assert s % tm == 0
    return pl.pallas_call(
        _final_norm_kernel,
        grid=(bsz, s // tm),
        in_specs=[pl.BlockSpec((None, tm, d), lambda b, i: (b, i, 0)),
                  _const_spec(g.shape)],
        out_specs=pl.BlockSpec((None, tm, d), lambda b, i: (b, i, 0)),
        out_shape=jax.ShapeDtypeStruct(x.shape, F32),
        compiler_params=pltpu.CompilerParams(
            dimension_semantics=("arbitrary", "arbitrary"),
            vmem_limit_bytes=VMEM_LIMIT),
        name="final_norm",
    )(x, g)


def _block_diag(w):
    n, a, b = w.shape
    eye = jnp.eye(n, dtype=w.dtype)
    return jnp.einsum('nde,nm->ndme', w, eye).reshape(n * a, n * b)


def kernel(x, c, ada_w, ada_b, norm_mix_g, norm_ffn_g, w_in, rg_conv_w, rg_conv_b, rg_wa, rg_ba, rg_wx, rg_bx, rg_lambda, gla_wg2, gla_bg, gla_norm_g, w_out, ffn_w1, ffn_w3, ffn_w2, router_w, moe_w1, moe_w3, moe_w2, final_g):
    bsz, s, d = x.shape
    depth = ada_w.shape[0]
    d_rg = rg_conv_w.shape[2]
    rank = gla_wg2.shape[1]
    n_exp = router_w.shape[2]
    d_main = w_in.shape[2] - rank

    mod = _adaln_mod(c, ada_w, ada_b).reshape(depth, bsz, 6, d)

    for layer in range(depth):
        win_p = jnp.concatenate(
            [w_in[layer, :, :d_main],
             jnp.pad(w_in[layer, :, d_main:], ((0, 0), (0, LANES - rank)))], axis=1).astype(BF16)
        wg2_p = jnp.pad(gla_wg2[layer], ((0, LANES - rank), (0, 0))).astype(BF16)
        wgate = jnp.concatenate([_block_diag(rg_wa[layer]), _block_diag(rg_wx[layer])], axis=1).astype(BF16)
        bgate = jnp.concatenate([rg_ba[layer], rg_bx[layer]])[None, :]
        x = _mixer(x, mod[layer], norm_mix_g[layer][None, :], win_p, rg_conv_w[layer],
                   rg_conv_b[layer][None, :], wgate, bgate, rg_lambda[layer][None, :], wg2_p,
                   gla_bg[layer][None, :], gla_norm_g[layer][None, :], w_out[layer].astype(BF16))
        j = layer // 2
        ng = norm_ffn_g[layer][None, :]
        if layer % 2 == 0:
            x = _ffn(x, mod[layer], ng, ffn_w1[j].astype(BF16), ffn_w3[j].astype(BF16),
                     ffn_w2[j].astype(BF16))
        else:
            rw_p = jnp.pad(router_w[j], ((0, 0), (0, LANES - n_exp)))
            x = _moe(x, mod[layer], ng, rw_p, moe_w1[j].astype(BF16), moe_w3[j].astype(BF16),
                     moe_w2[j].astype(BF16))
    return _final_norm(x, final_g[None, :])
```

```python
import functools

import jax
import jax.numpy as jnp
from jax import lax
from jax.experimental import pallas as pl
from jax.experimental.pallas import tpu as pltpu

F32 = jnp.float32
BF16 = jnp.bfloat16

EPS = 1e-6
RG_C = 8.0
CONV_W = 4
RG_BLOCKS = 8
GLA_HEADS = 4
GLA_TAU = 16.0
GLA_CHUNK = 64
TOP_K = 2

LANES = 128
SUBLANES = 8
VMEM_LIMIT = 56 * 1024 * 1024

MIX_TS = 256
FFN_TM = 512
FFN_FC = 256
ROUTER_TR = 256
MOE_TM = 512


def _sigmoid(x):
    return 1.0 / (1.0 + jnp.exp(-x))


def _rms_mod(x, g, shift, scale):
    ms = jnp.mean(x * x, axis=-1, keepdims=True)
    y = x * lax.rsqrt(ms + EPS) * g
    return y * (1.0 + scale) + shift


def _log2(n):
    assert n > 0 and n & (n - 1) == 0, n
    return n.bit_length() - 1


def _idiv(x, n):
    return lax.shift_right_logical(x, _log2(n))


def _imod(x, n):
    _log2(n)
    return x & (n - 1)


def _shift_rows(x, d, fill, seg=None):
    rolled = pltpu.roll(x, d, axis=0)
    row = lax.broadcasted_iota(jnp.int32, x.shape, 0)
    if seg is not None:
        row = _imod(row, seg)
    return jnp.where(row >= d, rolled, fill)


def _mod_kernel(c_ref, w_ref, b_ref, o_ref):
    c = c_ref[...]
    ca = c * _sigmoid(c)
    o_ref[...] = jnp.dot(ca, w_ref[...], precision=lax.Precision.HIGHEST,
                         preferred_element_type=F32) + b_ref[...]


def _adaln_mod(c, ada_w, ada_b):
    depth, d, n = ada_w.shape
    bsz = c.shape[0]
    rows = -(-bsz // SUBLANES) * SUBLANES
    c_pad = jnp.pad(c, ((0, rows - bsz), (0, 0)))
    tn = 1536
    assert n % tn == 0
    out = pl.pallas_call(
        _mod_kernel,
        grid=(depth, n // tn),
        in_specs=[
            pl.BlockSpec((rows, d), lambda l, j: (0, 0)),
            pl.BlockSpec((None, d, tn), lambda l, j: (l, 0, j)),
            pl.BlockSpec((None, 1, tn), lambda l, j: (l, 0, j)),
        ],
        out_specs=pl.BlockSpec((None, rows, tn), lambda l, j: (l, 0, j)),
        out_shape=jax.ShapeDtypeStruct((depth, rows, n), F32),
        compiler_params=pltpu.CompilerParams(
            dimension_semantics=("arbitrary", "arbitrary"),
            vmem_limit_bytes=VMEM_LIMIT),
        name="adaln_mod",
    )(c_pad, ada_w, ada_b.reshape(depth, 1, n))
    return out[:, :bsz, :]


def _mixer_kernel(x_ref, mod_ref, ng_ref, win_ref, cw_ref, cb_ref, wg_ref, bgate_ref,
                  lam_ref, wg2_ref, bg_ref, gg_ref, wout_ref, o_ref,
                  tail_ref, hst_ref, gst_ref, *, d_rg, kd, vd):
    ts = x_ref.shape[0]
    dk = kd // GLA_HEADS
    dv = vd // GLA_HEADS
    s_idx = pl.program_id(1)

    @pl.when(s_idx == 0)
    def _():
        tail_ref[...] = jnp.zeros_like(tail_ref)
        hst_ref[...] = jnp.zeros_like(hst_ref)
        gst_ref[...] = jnp.zeros_like(gst_ref)

    x = x_ref[...]
    mod = mod_ref[...]
    h = _rms_mod(x, ng_ref[...], mod[0:1], mod[1:2]).astype(BF16)
    z = jnp.dot(h, win_ref[...], preferred_element_type=F32)

    o0 = 0
    rg_x = z[:, o0:o0 + d_rg]; o0 += d_rg
    rg_gate = z[:, o0:o0 + d_rg]; o0 += d_rg
    q = z[:, o0:o0 + kd]; o0 += kd
    k = z[:, o0:o0 + kd]; o0 += kd
    v = z[:, o0:o0 + vd]; o0 += vd
    g = z[:, o0:o0 + vd]; o0 += vd
    f_low = z[:, o0:o0 + LANES]

    cat = jnp.concatenate([tail_ref[...], rg_x], axis=0)
    cw = cw_ref[...]
    u = cb_ref[...] + cw[CONV_W - 1:CONV_W] * rg_x
    for tap in range(CONV_W - 1):
        sh = CONV_W - 1 - tap
        u = u + cw[tap:tap + 1] * pltpu.roll(cat, sh, axis=0)[SUBLANES:SUBLANES + ts]
    tail_ref[...] = rg_x[ts - SUBLANES:ts]

    gates = jnp.dot(u.astype(BF16), wg_ref[...], preferred_element_type=F32) + bgate_ref[...]
    r = _sigmoid(gates[:, :d_rg])
    i_gate = _sigmoid(gates[:, d_rg:])
    lam = lam_ref[...]
    sp = jnp.maximum(-lam, 0.0) + jnp.log1p(jnp.exp(-jnp.abs(lam)))
    log_a = -RG_C * r * sp
    a = jnp.exp(log_a)
    mult = jnp.sqrt(-jnp.tanh(log_a) * (a * a + 1.0))
    row = lax.broadcasted_iota(jnp.int32, (ts, d_rg), 0)
    mult = jnp.where((row == 0) & (s_idx == 0), 1.0, mult)
    b_term = mult * i_gate * u
    d = 1
    while d < ts:
        b_term = a * _shift_rows(b_term, d, 0.0) + b_term
        a = a * _shift_rows(a, d, 1.0)
        d *= 2
    h_rg = a * hst_ref[0:1] + b_term
    hst_ref[...] = jnp.broadcast_to(h_rg[ts - 1:ts], hst_ref.shape)
    gelu = 0.5 * rg_gate * (1.0 + jnp.tanh(0.7978845608028654 * (rg_gate + 0.044715 * rg_gate ** 3)))
    rg_out = h_rg * gelu

    fz = jnp.dot(f_low.astype(BF16), wg2_ref[...], preferred_element_type=F32) + bg_ref[...]
    log_f = (jnp.minimum(fz, 0.0) - jnp.log1p(jnp.exp(-jnp.abs(fz)))) * (1.0 / GLA_TAU)
    bcum = log_f
    d = 1
    while d < GLA_CHUNK:
        bcum = bcum + _shift_rows(bcum, d, 0.0, seg=GLA_CHUNK)
        d *= 2

    scale = dk ** -0.5
    nh = GLA_HEADS
    c_len = GLA_CHUNK
    r_i = lax.broadcasted_iota(jnp.int32, (nh * c_len, kd), 0)
    c_i = lax.broadcasted_iota(jnp.int32, (nh * c_len, kd), 1)
    head_mask = _idiv(r_i, c_len) == _idiv(c_i, dk)
    r_j = lax.broadcasted_iota(jnp.int32, (nh * c_len, c_len), 0)
    c_j = lax.broadcasted_iota(jnp.int32, (nh * c_len, c_len), 1)
    causal = c_j <= _imod(r_j, c_len)
    r_s = lax.broadcasted_iota(jnp.int32, (vd, kd), 0)
    c_s = lax.broadcasted_iota(jnp.int32, (vd, kd), 1)
    bd_mask = _idiv(r_s, dv) == _idiv(c_s, dk)

    state = gst_ref[...]
    o_chunks = []
    for ci in range(ts // c_len):
        lo = ci * c_len
        bc = bcum[lo:lo + c_len]
        qc = q[lo:lo + c_len] * scale
        kc = k[lo:lo + c_len]
        vc = v[lo:lo + c_len].astype(BF16)
        b_ref_row = bc[c_len // 2 - 1:c_len // 2]
        b_last = bc[c_len - 1:c_len]
        q_loc = (qc * jnp.exp(bc - b_ref_row)).astype(BF16)
        k_loc = (kc * jnp.exp(b_ref_row - bc)).astype(BF16)
        q_stack = jnp.where(head_mask, jnp.concatenate([q_loc] * nh, axis=0), jnp.zeros((), BF16))
        scores = lax.dot_general(q_stack, k_loc, (((1,), (1,)), ((), ())),
                                 preferred_element_type=F32)
        p = jnp.where(causal, scores, 0.0).astype(BF16)
        oi = jnp.dot(p, vc, preferred_element_type=F32)
        o_intra = jnp.concatenate(
            [oi[hh * c_len:(hh + 1) * c_len, hh * dv:(hh + 1) * dv] for hh in range(nh)], axis=1)
        q_b = (qc * jnp.exp(bc)).astype(BF16)
        o_inter = lax.dot_general(q_b, state.astype(BF16), (((1,), (1,)), ((), ())),
                                  preferred_element_type=F32)
        o_chunks.append(o_intra + o_inter)
        k_end = (kc * jnp.exp(b_last - bc)).astype(BF16)
        upd = lax.dot_general(vc, k_end, (((0,), (0,)), ((), ())),
                              preferred_element_type=F32)
        state = jnp.exp(b_last) * state + jnp.where(bd_mask, upd, 0.0)
    gst_ref[...] = state
    o = jnp.concatenate(o_chunks, axis=0)

    gg = gg_ref[...]
    heads = []
    for hh in range(nh):
        oh = o[:, hh * dv:(hh + 1) * dv]
        ms = jnp.mean(oh * oh, axis=-1, keepdims=True)
        heads.append(oh * lax.rsqrt(ms + EPS) * gg)
    gla_out = jnp.concatenate(heads, axis=1) * (g * _sigmoid(g))

    mix_in = jnp.concatenate([rg_out, gla_out], axis=1).astype(BF16)
    mix = jnp.dot(mix_in, wout_ref[...], preferred_element_type=F32)
    o_ref[...] = x + mod[2:3] * mix


def _const_spec(shape):
    nd = len(shape)
    return pl.BlockSpec(shape, lambda *_: (0,) * nd, pipeline_mode=pl.Buffered(1))


def _mixer(x, mod6, ng, win_p, cw, cb, wgate, bgate, lam, wg2_p, bg, gg, wout):
    bsz, s, d = x.shape
    d_rg = cw.shape[1]
    kd = bg.shape[1]
    vd = wout.shape[0] - d_rg
    ts = MIX_TS
    assert s % ts == 0 and ts % GLA_CHUNK == 0
    kern = functools.partial(_mixer_kernel, d_rg=d_rg, kd=kd, vd=vd)
    consts = [ng, win_p, cw, cb, wgate, bgate, lam, wg2_p, bg, gg, wout]
    return pl.pallas_call(
        kern,
        grid=(bsz, s // ts),
        in_specs=[
            pl.BlockSpec((None, ts, d), lambda b, i: (b, i, 0)),
            pl.BlockSpec((None, 6, d), lambda b, i: (b, 0, 0)),
        ] + [_const_spec(a.shape) for a in consts],
        out_specs=pl.BlockSpec((None, ts, d), lambda b, i: (b, i, 0)),
        out_shape=jax.ShapeDtypeStruct(x.shape, F32),
        scratch_shapes=[
            pltpu.VMEM((SUBLANES, d_rg), F32),
            pltpu.VMEM((SUBLANES, d_rg), F32),
            pltpu.VMEM((vd, kd), F32),
        ],
        compiler_params=pltpu.CompilerParams(
            dimension_semantics=("arbitrary", "arbitrary"),
            vmem_limit_bytes=VMEM_LIMIT),
        name="mixer",
    )(x, mod6, *consts)


def _swiglu_acc(h, w1_ref, w3_ref, w2_ref, acc):
    f = w1_ref.shape[-1]
    for j in range(f // FFN_FC):
        sl = slice(j * FFN_FC, (j + 1) * FFN_FC)
        a = jnp.dot(h, w1_ref[:, sl], preferred_element_type=F32)
        b = jnp.dot(h, w3_ref[:, sl], preferred_element_type=F32)
        p = (a * _sigmoid(a) * b).astype(BF16)
        acc = acc + jnp.dot(p, w2_ref[sl, :], preferred_element_type=F32)
    return acc


def _ffn_kernel(x_ref, mod_ref, ng_ref, w1_ref, w3_ref, w2_ref, o_ref):
    x = x_ref[...]
    mod = mod_ref[...]
    h = _rms_mod(x, ng_ref[...], mod[3:4], mod[4:5]).astype(BF16)
    acc = _swiglu_acc(h, w1_ref, w3_ref, w2_ref, jnp.zeros(x.shape, F32))
    o_ref[...] = x + mod[5:6] * acc


def _ffn(x, mod6, ng, w1, w3, w2):
    bsz, s, d = x.shape
    f = w1.shape[1]
    tm = FFN_TM
    assert s % tm == 0 and f % FFN_FC == 0
    return pl.pallas_call(
        _ffn_kernel,
        grid=(bsz, s // tm),
        in_specs=[
            pl.BlockSpec((None, tm, d), lambda b, i: (b, i, 0)),
            pl.BlockSpec((None, 6, d), lambda b, i: (b, 0, 0)),
            _const_spec(ng.shape), _const_spec(w1.shape), _const_spec(w3.shape),
            _const_spec(w2.shape),
        ],
        out_specs=pl.BlockSpec((None, tm, d), lambda b, i: (b, i, 0)),
        out_shape=jax.ShapeDtypeStruct(x.shape, F32),
        compiler_params=pltpu.CompilerParams(
            dimension_semantics=("arbitrary", "arbitrary"),
            vmem_limit_bytes=VMEM_LIMIT),
        name="ffn_dense",
    )(x, mod6, ng, w1, w3, w2)


def _top2(logits, n_exp):
    lane = lax.broadcasted_iota(jnp.int32, logits.shape, 1).astype(F32)
    neg = jnp.float32(-jnp.inf)
    lg = jnp.where(lane < n_exp, logits, neg)
    m1 = jnp.max(lg, axis=-1, keepdims=True)
    i1 = jnp.min(jnp.where(lg == m1, lane, float(LANES)), axis=-1, keepdims=True)
    lg2 = jnp.where(lane == i1, neg, lg)
    m2 = jnp.max(lg2, axis=-1, keepdims=True)
    i2 = jnp.min(jnp.where(lg2 == m2, lane, float(LANES)), axis=-1, keepdims=True)
    ex = jnp.exp(m2 - m1)
    return i1, i2, 1.0 / (1.0 + ex), ex / (1.0 + ex)


def _router_kernel(x_ref, mod_ref, ng_ref, rw_ref, h_ref, info_ref, cnt_ref, carry_ref, *, n_exp):
    first = (pl.program_id(0) == 0) & (pl.program_id(1) == 0)

    @pl.when(first)
    def _():
        carry_ref[...] = jnp.zeros_like(carry_ref)

    x = x_ref[...]
    mod = mod_ref[...]
    h = _rms_mod(x, ng_ref[...], mod[3:4], mod[4:5])
    h_ref[...] = h
    logits = jnp.dot(h, rw_ref[...], precision=lax.Precision.HIGHEST,
                     preferred_element_type=F32)
    i1, i2, p1, p2 = _top2(logits, n_exp)
    tr = x.shape[0]
    lane = lax.broadcasted_iota(jnp.int32, (tr, LANES), 1).astype(F32)
    oh1 = lane == i1
    oh2 = lane == i2
    sel = jnp.where(oh1, 1.0, 0.0) + jnp.where(oh2, 1.0, 0.0)
    r_t = lax.broadcasted_iota(jnp.int32, (tr, tr), 0)
    c_t = lax.broadcasted_iota(jnp.int32, (tr, tr), 1)
    tri = jnp.where(r_t > c_t, 1.0, 0.0).astype(BF16)
    excl = jnp.dot(tri, sel.astype(BF16), preferred_element_type=F32)
    base = carry_ref[0:1] + excl
    rank1 = jnp.sum(jnp.where(oh1, base, 0.0), axis=-1, keepdims=True)
    rank2 = jnp.sum(jnp.where(oh2, base, 0.0), axis=-1, keepdims=True)
    info = jnp.where(lane == 0.0, i1, 0.0)
    for col, val in ((1.0, i2), (2.0, rank1), (3.0, rank2), (4.0, p1), (5.0, p2)):
        info = jnp.where(lane == col, val, info)
    info_ref[...] = info
    carry = carry_ref[...] + jnp.sum(sel, axis=0, keepdims=True)
    carry_ref[...] = carry
    cnt_ref[...] = carry


def _router(x, mod6, ng, rw_p, n_exp):
    bsz, s, d = x.shape
    tr = ROUTER_TR
    assert s % tr == 0
    nblk = s // tr
    t = bsz * s
    kern = functools.partial(_router_kernel, n_exp=n_exp)
    return pl.pallas_call(
        kern,
        grid=(bsz, nblk),
        in_specs=[
            pl.BlockSpec((None, tr, d), lambda b, i: (b, i, 0)),
            pl.BlockSpec((None, 6, d), lambda b, i: (b, 0, 0)),
            _const_spec(ng.shape), _const_spec(rw_p.shape),
        ],
        out_specs=[
            pl.BlockSpec((tr, d), lambda b, i: (b * nblk + i, 0)),
            pl.BlockSpec((tr, LANES), lambda b, i: (b * nblk + i, 0)),
            pl.BlockSpec((SUBLANES, LANES), lambda b, i: (0, 0)),
        ],
        out_shape=[
            jax.ShapeDtypeStruct((t, d), F32),
            jax.ShapeDtypeStruct((t, LANES), F32),
            jax.ShapeDtypeStruct((SUBLANES, LANES), F32),
        ],
        scratch_shapes=[pltpu.VMEM((SUBLANES, LANES), F32)],
        compiler_params=pltpu.CompilerParams(
            dimension_semantics=("arbitrary", "arbitrary"),
            vmem_limit_bytes=VMEM_LIMIT),
        name="moe_router",
    )(x, mod6, ng, rw_p)


def _row_gather_start(src_hbm, row, dst, dst_row, sem):
    pltpu.make_async_copy(src_hbm.at[pl.ds(row, 1)], dst.at[pl.ds(dst_row, 1)], sem).start()


def _row_gather_wait(src_hbm, dst, sem):
    pltpu.make_async_copy(src_hbm.at[pl.ds(0, dst.shape[0])], dst, sem).wait()


def _expert_kernel(te_ref, nt_ref, src_ref, h_hbm, w1_ref, w3_ref, w2_ref, o_ref, xbuf, sem):
    i = pl.program_id(0)
    n_tiles = nt_ref[0]
    tm = xbuf.shape[1]

    def issue(tile, slot):
        base = tile * tm

        def body(r, carry):
            _row_gather_start(h_hbm, src_ref[base + r], xbuf.at[slot], r, sem.at[slot])
            return carry

        lax.fori_loop(0, tm, body, 0, unroll=8)

    @pl.when(i == 0)
    def _():
        issue(0, 0)

    @pl.when(i + 1 < n_tiles)
    def _():
        issue(i + 1, (i + 1) % 2)

    @pl.when(i < n_tiles)
    def _():
        slot = i % 2
        _row_gather_wait(h_hbm, xbuf.at[slot], sem.at[slot])
        xs = xbuf[slot].astype(BF16)
        o_ref[...] = _swiglu_acc(xs, w1_ref, w3_ref, w2_ref, jnp.zeros(o_ref.shape, F32))

    @pl.when(i >= n_tiles)
    def _():
        o_ref[...] = jnp.zeros_like(o_ref)


def _experts(tile_expert, n_tiles, src_tok, h, w1, w3, w2, nt_max):
    t, d = h.shape
    f = w1.shape[2]
    tm = MOE_TM
    assert f % FFN_FC == 0
    grid_spec = pltpu.PrefetchScalarGridSpec(
        num_scalar_prefetch=3,
        grid=(nt_max,),
        in_specs=[
            pl.BlockSpec(memory_space=pl.ANY),
            pl.BlockSpec((None, d, f), lambda i, te, nt, src: (te[i], 0, 0)),
            pl.BlockSpec((None, d, f), lambda i, te, nt, src: (te[i], 0, 0)),
            pl.BlockSpec((None, f, d), lambda i, te, nt, src: (te[i], 0, 0)),
        ],
        out_specs=pl.BlockSpec((tm, d), lambda i, te, nt, src: (i, 0)),
        scratch_shapes=[
            pltpu.VMEM((2, tm, d), F32),
            pltpu.SemaphoreType.DMA((2,)),
        ],
    )
    return pl.pallas_call(
        _expert_kernel,
        grid_spec=grid_spec,
        out_shape=jax.ShapeDtypeStruct((nt_max * tm, d), F32),
        compiler_params=pltpu.CompilerParams(
            dimension_semantics=("arbitrary",),
            vmem_limit_bytes=VMEM_LIMIT),
        name="moe_experts",
    )(tile_expert, n_tiles, src_tok, h, w1, w3, w2)


def _combine_kernel(d1_ref, d2_ref, x_ref, mod_ref, info_ref, fg_ref, y_hbm, o_ref, ybuf, sem,
                    *, final_norm):
    i = pl.program_id(0)
    n = pl.num_programs(0)
    tc = x_ref.shape[0]

    def issue(tile, slot):
        base = tile * tc

        def body(r, carry):
            _row_gather_start(y_hbm, d1_ref[base + r], ybuf.at[slot, 0], r, sem.at[slot])
            _row_gather_start(y_hbm, d2_ref[base + r], ybuf.at[slot, 1], r, sem.at[slot])
            return carry

        lax.fori_loop(0, tc, body, 0, unroll=8)

    @pl.when(i == 0)
    def _():
        issue(0, 0)

    @pl.when(i + 1 < n)
    def _():
        issue(i + 1, (i + 1) % 2)

    slot = i % 2
    _row_gather_wait(y_hbm, ybuf.at[slot, 0], sem.at[slot])
    _row_gather_wait(y_hbm, ybuf.at[slot, 1], sem.at[slot])
    info = info_ref[...]
    y = info[:, 4:5] * ybuf[slot, 0] + info[:, 5:6] * ybuf[slot, 1]
    out = x_ref[...] + mod_ref[5:6, :] * y
    if final_norm:
        ms = jnp.mean(out * out, axis=-1, keepdims=True)
        out = out * lax.rsqrt(ms + EPS) * fg_ref[...]
    o_ref[...] = out


def _combine(dest1, dest2, x, mod6, info, final_g, y, final_norm):
    bsz, s, d = x.shape
    tc = ROUTER_TR
    nblk = s // tc
    kern = functools.partial(_combine_kernel, final_norm=final_norm)
    grid_spec = pltpu.PrefetchScalarGridSpec(
        num_scalar_prefetch=2,
        grid=(bsz * nblk,),
        in_specs=[
            pl.BlockSpec((None, tc, d), lambda i, d1, d2: (i // nblk, i % nblk, 0)),
            pl.BlockSpec((None, 6, d), lambda i, d1, d2: (i // nblk, 0, 0)),
            pl.BlockSpec((tc, LANES), lambda i, d1, d2: (i, 0)),
            pl.BlockSpec(final_g.shape, lambda i, d1, d2: (0, 0)),
            pl.BlockSpec(memory_space=pl.ANY),
        ],
        out_specs=pl.BlockSpec((None, tc, d), lambda i, d1, d2: (i // nblk, i % nblk, 0)),
        scratch_shapes=[
            pltpu.VMEM((2, 2, tc, d), F32),
            pltpu.SemaphoreType.DMA((2,)),
        ],
    )
    return pl.pallas_call(
        kern,
        grid_spec=grid_spec,
        out_shape=jax.ShapeDtypeStruct(x.shape, F32),
        compiler_params=pltpu.CompilerParams(
            dimension_semantics=("arbitrary",),
            vmem_limit_bytes=VMEM_LIMIT),
        name="moe_combine",
    )(dest1, dest2, x, mod6, info, final_g, y)


def _moe(x, mod6, ng, rw_p, w1, w3, w2, final_g, final_norm):
    bsz, s, d = x.shape
    n_exp = w1.shape[0]
    t = bsz * s
    tm = MOE_TM
    nt_max = (TOP_K * t) // tm + n_exp
    h, info, cnt = _router(x, mod6, ng, rw_p, n_exp)

    counts = cnt[0, :n_exp].astype(jnp.int32)
    tiles_per = (counts + tm - 1) // tm
    tile_end = jnp.cumsum(tiles_per)
    row_start = (tile_end - tiles_per) * tm
    n_tiles = tile_end[-1]
    e1 = info[:, 0].astype(jnp.int32)
    e2 = info[:, 1].astype(jnp.int32)
    dest1 = row_start[e1] + info[:, 2].astype(jnp.int32)
    dest2 = row_start[e2] + info[:, 3].astype(jnp.int32)
    tok = jnp.arange(t, dtype=jnp.int32)
    src_tok = jnp.zeros((nt_max * tm,), jnp.int32).at[dest1].set(tok).at[dest2].set(tok)
    tile_ids = jnp.arange(nt_max, dtype=jnp.int32)
    live_ids = jnp.minimum(tile_ids, n_tiles - 1)
    tile_expert = jnp.sum((live_ids[:, None] >= tile_end[None, :]).astype(jnp.int32), axis=1)
    tile_expert = jnp.minimum(tile_expert, n_exp - 1)

    y = _experts(tile_expert, n_tiles[None].astype(jnp.int32), src_tok, h, w1, w3, w2, nt_max)
    return _combine(dest1, dest2, x, mod6, info, final_g, y, final_norm)


def _final_norm_kernel(x_ref, g_ref, o_ref):
    x = x_ref[...]
    ms = jnp.mean(x * x, axis=-1, keepdims=True)
    o_ref[...] = x * lax.rsqrt(ms + EPS) * g_ref[...]


def _final_norm(x, g):
    bsz, s, d = x.shape
    tm = FFN_TM
    assert s % tm == 0
    return pl.pallas_call(
        _final_norm_kernel,
        grid=(bsz, s // tm),
        in_specs=[pl.BlockSpec((None, tm, d), lambda b, i: (b, i, 0)),
                  _const_spec(g.shape)],
        out_specs=pl.BlockSpec((None, tm, d), lambda b, i: (b, i, 0)),
        out_shape=jax.ShapeDtypeStruct(x.shape, F32),
        compiler_params=pltpu.CompilerParams(
            dimension_semantics=("arbitrary", "arbitrary"),
            vmem_limit_bytes=VMEM_LIMIT),
        name="final_norm",
    )(x, g)


def _block_diag(w):
    n, a, b = w.shape
    eye = jnp.eye(n, dtype=w.dtype)
    return jnp.einsum('nde,nm->ndme', w, eye).reshape(n * a, n * b)


def kernel(x, c, ada_w, ada_b, norm_mix_g, norm_ffn_g, w_in, rg_conv_w, rg_conv_b, rg_wa, rg_ba, rg_wx, rg_bx, rg_lambda, gla_wg2, gla_bg, gla_norm_g, w_out, ffn_w1, ffn_w3, ffn_w2, router_w, moe_w1, moe_w3, moe_w2, final_g):
    bsz, s, d = x.shape
    depth = ada_w.shape[0]
    d_rg = rg_conv_w.shape[2]
    rank = gla_wg2.shape[1]
    n_exp = router_w.shape[2]
    d_main = w_in.shape[2] - rank

    mod = _adaln_mod(c, ada_w, ada_b).reshape(depth, bsz, 6, d)

    for layer in range(depth):
        win_p = jnp.concatenate(
            [w_in[layer, :, :d_main],
             jnp.pad(w_in[layer, :, d_main:], ((0, 0), (0, LANES - rank)))], axis=1).astype(BF16)
        wg2_p = jnp.pad(gla_wg2[layer], ((0, LANES - rank), (0, 0))).astype(BF16)
        wgate = jnp.concatenate([_block_diag(rg_wa[layer]), _block_diag(rg_wx[layer])], axis=1).astype(BF16)
        bgate = jnp.concatenate([rg_ba[layer], rg_bx[layer]])[None, :]
        x = _mixer(x, mod[layer], norm_mix_g[layer][None, :], win_p, rg_conv_w[layer],
                   rg_conv_b[layer][None, :], wgate, bgate, rg_lambda[layer][None, :], wg2_p,
                   gla_bg[layer][None, :], gla_norm_g[layer][None, :], w_out[layer].astype(BF16))
        j = layer // 2
        ng = norm_ffn_g[layer][None, :]
        if layer % 2 == 0:
            x = _ffn(x, mod[layer], ng, ffn_w1[j].astype(BF16), ffn_w3[j].astype(BF16),
                     ffn_w2[j].astype(BF16))
        else:
            rw_p = jnp.pad(router_w[j], ((0, 0), (0, LANES - n_exp)))
            last = layer == depth - 1
            x = _moe(x, mod[layer], ng, rw_p, moe_w1[j].astype(BF16), moe_w3[j].astype(BF16),
                     moe_w2[j].astype(BF16), final_g[None, :], final_norm=last)
            if last:
                return x
    return _final_norm(x, final_g[None, :])
```

```python
import functools

import jax
import jax.numpy as jnp
from jax import lax
from jax.experimental import pallas as pl
from jax.experimental.pallas import tpu as pltpu

F32 = jnp.float32
BF16 = jnp.bfloat16

EPS = 1e-6
RG_C = 8.0
CONV_W = 4
RG_BLOCKS = 8
GLA_HEADS = 4
GLA_TAU = 16.0
GLA_CHUNK = 64
TOP_K = 2

LANES = 128
SUBLANES = 8
VMEM_LIMIT = 56 * 1024 * 1024

MIX_TS = 256
MIX_NSEQ = 2
FFN_TM = 512
FFN_FC = 256
ROUTER_TR = 256
MOE_TM = 512


def _sigmoid(x):
    return 1.0 / (1.0 + jnp.exp(-x))


def _rms_mod(x, g, shift, scale):
    ms = jnp.mean(x * x, axis=-1, keepdims=True)
    return (x * lax.rsqrt(ms + EPS)) * (g * (1.0 + scale)) + shift


def _log2(n):
    assert n > 0 and n & (n - 1) == 0, n
    return n.bit_length() - 1


def _idiv(x, n):
    return lax.shift_right_logical(x, _log2(n))


def _imod(x, n):
    _log2(n)
    return x & (n - 1)


def _shift_rows(x, d, fill, seg=None):
    rolled = pltpu.roll(x, d, axis=0)
    row = lax.broadcasted_iota(jnp.int32, x.shape, 0)
    if seg is not None:
        row = _imod(row, seg)
    return jnp.where(row >= d, rolled, fill)


def _linear_scan(a, b, h0):
    rows, n = a.shape
    groups = rows // SUBLANES
    a3 = a.reshape(groups, SUBLANES, n)
    b3 = b.reshape(groups, SUBLANES, n)
    sub = lax.broadcasted_iota(jnp.int32, a3.shape, 1)
    d = 1
    while d < SUBLANES:
        keep = sub >= d
        b3 = a3 * jnp.where(keep, pltpu.roll(b3, d, axis=1), 0.0) + b3
        a3 = a3 * jnp.where(keep, pltpu.roll(a3, d, axis=1), 1.0)
        d *= 2
    carry = h0
    out = []
    for gi in range(groups):
        hg = a3[gi] * carry + b3[gi]
        out.append(hg)
        carry = hg[SUBLANES - 1:SUBLANES]
    return jnp.concatenate(out, axis=0)


def _mod_kernel(c_ref, w_ref, b_ref, o_ref):
    c = c_ref[...]
    ca = c * _sigmoid(c)
    o_ref[...] = jnp.dot(ca, w_ref[...], precision=lax.Precision.HIGHEST,
                         preferred_element_type=F32) + b_ref[...]


def _adaln_mod(c, ada_w, ada_b):
    depth, d, n = ada_w.shape
    bsz = c.shape[0]
    rows = -(-bsz // SUBLANES) * SUBLANES
    c_pad = jnp.pad(c, ((0, rows - bsz), (0, 0)))
    tn = 1536
    assert n % tn == 0
    out = pl.pallas_call(
        _mod_kernel,
        grid=(depth, n // tn),
        in_specs=[
            pl.BlockSpec((rows, d), lambda l, j: (0, 0)),
            pl.BlockSpec((None, d, tn), lambda l, j: (l, 0, j)),
            pl.BlockSpec((None, 1, tn), lambda l, j: (l, 0, j)),
        ],
        out_specs=pl.BlockSpec((None, rows, tn), lambda l, j: (l, 0, j)),
        out_shape=jax.ShapeDtypeStruct((depth, rows, n), F32),
        compiler_params=pltpu.CompilerParams(
            dimension_semantics=("arbitrary", "arbitrary"),
            vmem_limit_bytes=VMEM_LIMIT),
        name="adaln_mod",
    )(c_pad, ada_w, ada_b.reshape(depth, 1, n))
    return out[:, :bsz, :]


def _mixer_kernel(x_ref, mod_ref, *rest, **dims):
    consts, (o_ref, tail_ref, hst_ref, gst_ref) = rest[:-4], rest[-4:]
    s_idx = pl.program_id(1)

    @pl.when(s_idx == 0)
    def _():
        tail_ref[...] = jnp.zeros_like(tail_ref)
        hst_ref[...] = jnp.zeros_like(hst_ref)
        gst_ref[...] = jnp.zeros_like(gst_ref)

    for k in range(x_ref.shape[0]):
        _mixer_seq(x_ref.at[k], mod_ref.at[k], *consts, o_ref.at[k], tail_ref.at[k],
                   hst_ref.at[k], gst_ref.at[k], s_idx, **dims)


def _mixer_seq(x_ref, mod_ref, ng_ref, win_ref, cw_ref, cb_ref, wg_ref, bgate_ref,
               lam_ref, wg2_ref, bg_ref, gg_ref, wout_ref, o_ref,
               tail_ref, hst_ref, gst_ref, s_idx, *, d_rg, kd, vd):
    ts = x_ref.shape[0]
    dk = kd // GLA_HEADS
    dv = vd // GLA_HEADS

    x = x_ref[...]
    mod = mod_ref[...]
    h = _rms_mod(x, ng_ref[...], mod[0:1], mod[1:2]).astype(BF16)
    z = jnp.dot(h, win_ref[...], preferred_element_type=F32)

    o0 = 0
    rg_x = z[:, o0:o0 + d_rg]; o0 += d_rg
    rg_gate = z[:, o0:o0 + d_rg]; o0 += d_rg
    q = z[:, o0:o0 + kd]; o0 += kd
    k = z[:, o0:o0 + kd]; o0 += kd
    v = z[:, o0:o0 + vd]; o0 += vd
    g = z[:, o0:o0 + vd]; o0 += vd
    f_low = z[:, o0:o0 + LANES]

    cat = jnp.concatenate([tail_ref[...], rg_x], axis=0)
    cw = cw_ref[...]
    u = cb_ref[...] + cw[CONV_W - 1:CONV_W] * rg_x
    for tap in range(CONV_W - 1):
        sh = CONV_W - 1 - tap
        u = u + cw[tap:tap + 1] * pltpu.roll(cat, sh, axis=0)[SUBLANES:SUBLANES + ts]
    tail_ref[...] = rg_x[ts - SUBLANES:ts]

    gates = jnp.dot(u.astype(BF16), wg_ref[...], preferred_element_type=F32) + bgate_ref[...]
    r = _sigmoid(gates[:, :d_rg])
    i_gate = _sigmoid(gates[:, d_rg:])
    lam = lam_ref[...]
    sp = jnp.maximum(-lam, 0.0) + jnp.log1p(jnp.exp(-jnp.abs(lam)))
    log_a = -RG_C * r * sp
    a = jnp.exp(log_a)
    mult = jnp.sqrt(-jnp.tanh(log_a) * (a * a + 1.0))
    row = lax.broadcasted_iota(jnp.int32, (ts, d_rg), 0)
    mult = jnp.where((row == 0) & (s_idx == 0), 1.0, mult)
    b_term = mult * i_gate * u
    h_rg = _linear_scan(a, b_term, hst_ref[0:1])
    hst_ref[...] = jnp.broadcast_to(h_rg[ts - 1:ts], hst_ref.shape)
    gelu = 0.5 * rg_gate * (1.0 + jnp.tanh(0.7978845608028654 * (rg_gate + 0.044715 * rg_gate ** 3)))
    rg_out = h_rg * gelu

    fz = jnp.dot(f_low.astype(BF16), wg2_ref[...], preferred_element_type=F32) + bg_ref[...]
    log_f = (jnp.minimum(fz, 0.0) - jnp.log1p(jnp.exp(-jnp.abs(fz)))) * (1.0 / GLA_TAU)
    bcum = log_f
    d = 1
    while d < GLA_CHUNK:
        bcum = bcum + _shift_rows(bcum, d, 0.0, seg=GLA_CHUNK)
        d *= 2

    scale = dk ** -0.5
    nh = GLA_HEADS
    c_len = GLA_CHUNK
    r_i = lax.broadcasted_iota(jnp.int32, (nh * c_len, kd), 0)
    c_i = lax.broadcasted_iota(jnp.int32, (nh * c_len, kd), 1)
    head_mask = _idiv(r_i, c_len) == _idiv(c_i, dk)
    r_j = lax.broadcasted_iota(jnp.int32, (nh * c_len, c_len), 0)
    c_j = lax.broadcasted_iota(jnp.int32, (nh * c_len, c_len), 1)
    causal = c_j <= _imod(r_j, c_len)
    r_s = lax.broadcasted_iota(jnp.int32, (vd, kd), 0)
    c_s = lax.broadcasted_iota(jnp.int32, (vd, kd), 1)
    bd_mask = _idiv(r_s, dv) == _idiv(c_s, dk)

    state = gst_ref[...]
    o_chunks = []
    for ci in range(ts // c_len):
        lo = ci * c_len
        bc = bcum[lo:lo + c_len]
        qc = q[lo:lo + c_len] * scale
        kc = k[lo:lo + c_len]
        vc = v[lo:lo + c_len].astype(BF16)
        b_ref_row = bc[c_len // 2 - 1:c_len // 2]
        b_last = bc[c_len - 1:c_len]
        q_loc = (qc * jnp.exp(bc - b_ref_row)).astype(BF16)
        k_loc = (kc * jnp.exp(b_ref_row - bc)).astype(BF16)
        q_stack = jnp.where(head_mask, jnp.concatenate([q_loc] * nh, axis=0), jnp.zeros((), BF16))
        scores = lax.dot_general(q_stack, k_loc, (((1,), (1,)), ((), ())),
                                 preferred_element_type=F32)
        p = jnp.where(causal, scores, 0.0).astype(BF16)
        oi = jnp.dot(p, vc, preferred_element_type=F32)
        o_intra = jnp.concatenate(
            [oi[hh * c_len:(hh + 1) * c_len, hh * dv:(hh + 1) * dv] for hh in range(nh)], axis=1)
        q_b = (qc * jnp.exp(bc)).astype(BF16)
        o_inter = lax.dot_general(q_b, state.astype(BF16), (((1,), (1,)), ((), ())),
                                  preferred_element_type=F32)
        o_chunks.append(o_intra + o_inter)
        k_end = (kc * jnp.exp(b_last - bc)).astype(BF16)
        upd = lax.dot_general(vc, k_end, (((0,), (0,)), ((), ())),
                              preferred_element_type=F32)
        state = jnp.exp(b_last) * state + jnp.where(bd_mask, upd, 0.0)
    gst_ref[...] = state
    o = jnp.concatenate(o_chunks, axis=0)

    gg = gg_ref[...]
    heads = []
    for hh in range(nh):
        oh = o[:, hh * dv:(hh + 1) * dv]
        ms = jnp.mean(oh * oh, axis=-1, keepdims=True)
        heads.append(oh * lax.rsqrt(ms + EPS) * gg)
    gla_out = jnp.concatenate(heads, axis=1) * (g * _sigmoid(g))

    mix_in = jnp.concatenate([rg_out, gla_out], axis=1).astype(BF16)
    mix = jnp.dot(mix_in, wout_ref[...], preferred_element_type=F32)
    o_ref[...] = x + mod[2:3] * mix


def _const_spec(shape):
    nd = len(shape)
    return pl.BlockSpec(shape, lambda *_: (0,) * nd, pipeline_mode=pl.Buffered(1))


def _layer_spec(stacked, layer):
    rest = stacked.shape[1:]
    return pl.BlockSpec((None,) + rest, lambda *_: (layer,) + (0,) * len(rest),
                        pipeline_mode=pl.Buffered(1))


def _mixer(x, mod6, layer, consts):
    bsz, s, d = x.shape
    d_rg = consts[2].shape[2]
    kd = consts[8].shape[2]
    vd = consts[10].shape[1] - d_rg
    ts = MIX_TS
    nseq = MIX_NSEQ if bsz % MIX_NSEQ == 0 else 1
    assert s % ts == 0 and ts % GLA_CHUNK == 0
    kern = functools.partial(_mixer_kernel, d_rg=d_rg, kd=kd, vd=vd)
    return pl.pallas_call(
        kern,
        grid=(bsz // nseq, s // ts),
        in_specs=[
            pl.BlockSpec((nseq, ts, d), lambda b, i: (b, i, 0)),
            pl.BlockSpec((nseq, 6, d), lambda b, i: (b, 0, 0)),
        ] + [_layer_spec(a, layer) for a in consts],
        out_specs=pl.BlockSpec((nseq, ts, d), lambda b, i: (b, i, 0)),
        out_shape=jax.ShapeDtypeStruct(x.shape, F32),
        scratch_shapes=[
            pltpu.VMEM((nseq, SUBLANES, d_rg), F32),
            pltpu.VMEM((nseq, SUBLANES, d_rg), F32),
            pltpu.VMEM((nseq, vd, kd), F32),
        ],
        compiler_params=pltpu.CompilerParams(
            dimension_semantics=("arbitrary", "arbitrary"),
            vmem_limit_bytes=VMEM_LIMIT),
        name="mixer",
    )(x, mod6, *consts)


def _swiglu_acc(h, w1_ref, w3_ref, w2_ref, acc):
    f = w1_ref.shape[-1]
    for j in range(f // FFN_FC):
        sl = slice(j * FFN_FC, (j + 1) * FFN_FC)
        a = jnp.dot(h, w1_ref[:, sl], preferred_element_type=F32)
        b = jnp.dot(h, w3_ref[:, sl], preferred_element_type=F32)
        p = (a * _sigmoid(a) * b).astype(BF16)
        acc = acc + jnp.dot(p, w2_ref[sl, :], preferred_element_type=F32)
    return acc


def _ffn_kernel(x_ref, mod_ref, ng_ref, w1_ref, w3_ref, w2_ref, o_ref):
    x = x_ref[...]
    mod = mod_ref[...]
    h = _rms_mod(x, ng_ref[...], mod[3:4], mod[4:5]).astype(BF16)
    acc = _swiglu_acc(h, w1_ref, w3_ref, w2_ref, jnp.zeros(x.shape, F32))
    o_ref[...] = x + mod[5:6] * acc


def _ffn(x, mod6, ng, layer, w1, w3, w2, j):
    bsz, s, d = x.shape
    f = w1.shape[2]
    tm = FFN_TM
    assert s % tm == 0 and f % FFN_FC == 0
    return pl.pallas_call(
        _ffn_kernel,
        grid=(bsz, s // tm),
        in_specs=[
            pl.BlockSpec((None, tm, d), lambda b, i: (b, i, 0)),
            pl.BlockSpec((None, 6, d), lambda b, i: (b, 0, 0)),
            _layer_spec(ng, layer), _layer_spec(w1, j), _layer_spec(w3, j), _layer_spec(w2, j),
        ],
        out_specs=pl.BlockSpec((None, tm, d), lambda b, i: (b, i, 0)),
        out_shape=jax.ShapeDtypeStruct(x.shape, F32),
        compiler_params=pltpu.CompilerParams(
            dimension_semantics=("arbitrary", "arbitrary"),
            vmem_limit_bytes=VMEM_LIMIT),
        name="ffn_dense",
    )(x, mod6, ng, w1, w3, w2)


def _top2(logits, n_exp):
    lane = lax.broadcasted_iota(jnp.int32, logits.shape, 1).astype(F32)
    neg = jnp.float32(-jnp.inf)
    lg = jnp.where(lane < n_exp, logits, neg)
    m1 = jnp.max(lg, axis=-1, keepdims=True)
    i1 = jnp.min(jnp.where(lg == m1, lane, float(LANES)), axis=-1, keepdims=True)
    lg2 = jnp.where(lane == i1, neg, lg)
    m2 = jnp.max(lg2, axis=-1, keepdims=True)
    i2 = jnp.min(jnp.where(lg2 == m2, lane, float(LANES)), axis=-1, keepdims=True)
    ex = jnp.exp(m2 - m1)
    return i1, i2, 1.0 / (1.0 + ex), ex / (1.0 + ex)


def _router_kernel(x_ref, mod_ref, ng_ref, rw_ref, h_ref, info_ref, cnt_ref, carry_ref, *, n_exp):
    first = (pl.program_id(0) == 0) & (pl.program_id(1) == 0)

    @pl.when(first)
    def _():
        carry_ref[...] = jnp.zeros_like(carry_ref)

    x = x_ref[...]
    mod = mod_ref[...]
    h = _rms_mod(x, ng_ref[...], mod[3:4], mod[4:5])
    h_ref[...] = h
    logits = jnp.dot(h, rw_ref[...], precision=lax.Precision.HIGHEST,
                     preferred_element_type=F32)
    i1, i2, p1, p2 = _top2(logits, n_exp)
    tr = x.shape[0]
    lane = lax.broadcasted_iota(jnp.int32, (tr, LANES), 1).astype(F32)
    oh1 = lane == i1
    oh2 = lane == i2
    sel = jnp.where(oh1, 1.0, 0.0) + jnp.where(oh2, 1.0, 0.0)
    r_t = lax.broadcasted_iota(jnp.int32, (tr, tr), 0)
    c_t = lax.broadcasted_iota(jnp.int32, (tr, tr), 1)
    tri = jnp.where(r_t > c_t, 1.0, 0.0).astype(BF16)
    excl = jnp.dot(tri, sel.astype(BF16), preferred_element_type=F32)
    base = carry_ref[0:1] + excl
    rank1 = jnp.sum(jnp.where(oh1, base, 0.0), axis=-1, keepdims=True)
    rank2 = jnp.sum(jnp.where(oh2, base, 0.0), axis=-1, keepdims=True)
    info = jnp.where(lane == 0.0, i1, 0.0)
    for col, val in ((1.0, i2), (2.0, rank1), (3.0, rank2), (4.0, p1), (5.0, p2)):
        info = jnp.where(lane == col, val, info)
    info_ref[...] = info
    carry = carry_ref[...] + jnp.sum(sel, axis=0, keepdims=True)
    carry_ref[...] = carry
    cnt_ref[...] = carry


def _router(x, mod6, ng, layer, rw_p, j, n_exp):
    bsz, s, d = x.shape
    tr = ROUTER_TR
    assert s % tr == 0
    nblk = s // tr
    t = bsz * s
    kern = functools.partial(_router_kernel, n_exp=n_exp)
    return pl.pallas_call(
        kern,
        grid=(bsz, nblk),
        in_specs=[
            pl.BlockSpec((None, tr, d), lambda b, i: (b, i, 0)),
            pl.BlockSpec((None, 6, d), lambda b, i: (b, 0, 0)),
            _layer_spec(ng, layer), _layer_spec(rw_p, j),
        ],
        out_specs=[
            pl.BlockSpec((tr, d), lambda b, i: (b * nblk + i, 0)),
            pl.BlockSpec((tr, LANES), lambda b, i: (b * nblk + i, 0)),
            pl.BlockSpec((SUBLANES, LANES), lambda b, i: (0, 0)),
        ],
        out_shape=[
            jax.ShapeDtypeStruct((t, d), F32),
            jax.ShapeDtypeStruct((t, LANES), F32),
            jax.ShapeDtypeStruct((SUBLANES, LANES), F32),
        ],
        scratch_shapes=[pltpu.VMEM((SUBLANES, LANES), F32)],
        compiler_params=pltpu.CompilerParams(
            dimension_semantics=("arbitrary", "arbitrary"),
            vmem_limit_bytes=VMEM_LIMIT),
        name="moe_router",
    )(x, mod6, ng, rw_p)


def _row_gather_start(src_hbm, row, dst, dst_row, sem):
    pltpu.make_async_copy(src_hbm.at[pl.ds(row, 1)], dst.at[pl.ds(dst_row, 1)], sem).start()


def _row_gather_wait(src_hbm, dst, sem):
    pltpu.make_async_copy(src_hbm.at[pl.ds(0, dst.shape[0])], dst, sem).wait()


def _expert_kernel(te_ref, nt_ref, src_ref, h_hbm, w1_ref, w3_ref, w2_ref, o_ref, xbuf, sem):
    i = pl.program_id(0)
    n_tiles = nt_ref[0]
    tm = xbuf.shape[1]

    def issue(tile, slot):
        base = tile * tm

        def body(r, carry):
            _row_gather_start(h_hbm, src_ref[base + r], xbuf.at[slot], r, sem.at[slot])
            return carry

        lax.fori_loop(0, tm, body, 0, unroll=8)

    @pl.when(i == 0)
    def _():
        issue(0, 0)

    @pl.when(i + 1 < n_tiles)
    def _():
        issue(i + 1, (i + 1) % 2)

    @pl.when(i < n_tiles)
    def _():
        slot = i % 2
        _row_gather_wait(h_hbm, xbuf.at[slot], sem.at[slot])
        xs = xbuf[slot].astype(BF16)
        o_ref[...] = _swiglu_acc(xs, w1_ref, w3_ref, w2_ref, jnp.zeros(o_ref.shape, F32))

    @pl.when(i >= n_tiles)
    def _():
        o_ref[...] = jnp.zeros_like(o_ref)


def _experts(tile_expert, n_tiles, src_tok, h, w1, w3, w2, j, nt_max):
    t, d = h.shape
    f = w1.shape[3]
    tm = MOE_TM
    assert f % FFN_FC == 0
    grid_spec = pltpu.PrefetchScalarGridSpec(
        num_scalar_prefetch=3,
        grid=(nt_max,),
        in_specs=[
            pl.BlockSpec(memory_space=pl.ANY),
            pl.BlockSpec((None, None, d, f), lambda i, te, nt, src: (j, te[i], 0, 0)),
            pl.BlockSpec((None, None, d, f), lambda i, te, nt, src: (j, te[i], 0, 0)),
            pl.BlockSpec((None, None, f, d), lambda i, te, nt, src: (j, te[i], 0, 0)),
        ],
        out_specs=pl.BlockSpec((tm, d), lambda i, te, nt, src: (i, 0)),
        scratch_shapes=[
            pltpu.VMEM((2, tm, d), F32),
            pltpu.SemaphoreType.DMA((2,)),
        ],
    )
    return pl.pallas_call(
        _expert_kernel,
        grid_spec=grid_spec,
        out_shape=jax.ShapeDtypeStruct((nt_max * tm, d), F32),
        compiler_params=pltpu.CompilerParams(
            dimension_semantics=("arbitrary",),
            vmem_limit_bytes=VMEM_LIMIT),
        name="moe_experts",
    )(tile_expert, n_tiles, src_tok, h, w1, w3, w2)


def _combine_kernel(d1_ref, d2_ref, x_ref, mod_ref, info_ref, fg_ref, y_hbm, o_ref, ybuf, sem,
                    *, final_norm):
    i = pl.program_id(0)
    n = pl.num_programs(0)
    tc = x_ref.shape[0]

    def issue(tile, slot):
        base = tile * tc

        def body(r, carry):
            _row_gather_start(y_hbm, d1_ref[base + r], ybuf.at[slot, 0], r, sem.at[slot])
            _row_gather_start(y_hbm, d2_ref[base + r], ybuf.at[slot, 1], r, sem.at[slot])
            return carry

        lax.fori_loop(0, tc, body, 0, unroll=8)

    @pl.when(i == 0)
    def _():
        issue(0, 0)

    @pl.when(i + 1 < n)
    def _():
        issue(i + 1, (i + 1) % 2)

    slot = i % 2
    _row_gather_wait(y_hbm, ybuf.at[slot, 0], sem.at[slot])
    _row_gather_wait(y_hbm, ybuf.at[slot, 1], sem.at[slot])
    info = info_ref[...]
    y = info[:, 4:5] * ybuf[slot, 0] + info[:, 5:6] * ybuf[slot, 1]
    out = x_ref[...] + mod_ref[5:6, :] * y
    if final_norm:
        ms = jnp.mean(out * out, axis=-1, keepdims=True)
        out = out * lax.rsqrt(ms + EPS) * fg_ref[...]
    o_ref[...] = out


def _combine(dest1, dest2, x, mod6, info, final_g, y, final_norm):
    bsz, s, d = x.shape
    tc = ROUTER_TR
    nblk = s // tc
    kern = functools.partial(_combine_kernel, final_norm=final_norm)
    grid_spec = pltpu.PrefetchScalarGridSpec(
        num_scalar_prefetch=2,
        grid=(bsz * nblk,),
        in_specs=[
            pl.BlockSpec((None, tc, d), lambda i, d1, d2: (i // nblk, i % nblk, 0)),
            pl.BlockSpec((None, 6, d), lambda i, d1, d2: (i // nblk, 0, 0)),
            pl.BlockSpec((tc, LANES), lambda i, d1, d2: (i, 0)),
            pl.BlockSpec(final_g.shape, lambda i, d1, d2: (0, 0)),
            pl.BlockSpec(memory_space=pl.ANY),
        ],
        out_specs=pl.BlockSpec((None, tc, d), lambda i, d1, d2: (i // nblk, i % nblk, 0)),
        scratch_shapes=[
            pltpu.VMEM((2, 2, tc, d), F32),
            pltpu.SemaphoreType.DMA((2,)),
        ],
    )
    return pl.pallas_call(
        kern,
        grid_spec=grid_spec,
        out_shape=jax.ShapeDtypeStruct(x.shape, F32),
        compiler_params=pltpu.CompilerParams(
            dimension_semantics=("arbitrary",),
            vmem_limit_bytes=VMEM_LIMIT),
        name="moe_combine",
    )(dest1, dest2, x, mod6, info, final_g, y)


def _moe(x, mod6, ng, layer, rw_p, w1, w3, w2, j, final_g, final_norm):
    bsz, s, d = x.shape
    n_exp = w1.shape[1]
    t = bsz * s
    tm = MOE_TM
    nt_max = (TOP_K * t) // tm + n_exp
    h, info, cnt = _router(x, mod6, ng, layer, rw_p, j, n_exp)

    counts = cnt[0, :n_exp].astype(jnp.int32)
    tiles_per = (counts + tm - 1) // tm
    tile_end = jnp.cumsum(tiles_per)
    row_start = (tile_end - tiles_per) * tm
    n_tiles = tile_end[-1]
    e1 = info[:, 0].astype(jnp.int32)
    e2 = info[:, 1].astype(jnp.int32)
    dest1 = row_start[e1] + info[:, 2].astype(jnp.int32)
    dest2 = row_start[e2] + info[:, 3].astype(jnp.int32)
    tok = jnp.arange(t, dtype=jnp.int32)
    src_tok = jnp.zeros((nt_max * tm,), jnp.int32).at[jnp.concatenate([dest1, dest2])].set(
        jnp.concatenate([tok, tok]), unique_indices=True, mode='promise_in_bounds')
    tile_ids = jnp.arange(nt_max, dtype=jnp.int32)
    live_ids = jnp.minimum(tile_ids, n_tiles - 1)
    tile_expert = jnp.sum((live_ids[:, None] >= tile_end[None, :]).astype(jnp.int32), axis=1)
    tile_expert = jnp.minimum(tile_expert, n_exp - 1)

    y = _experts(tile_expert, n_tiles[None].astype(jnp.int32), src_tok, h, w1, w3, w2, j, nt_max)
    return _combine(dest1, dest2, x, mod6, info, final_g, y, final_norm)


def _final_norm_kernel(x_ref, g_ref, o_ref):
    x = x_ref[...]
    ms = jnp.mean(x * x, axis=-1, keepdims=True)
    o_ref[...] = x * lax.rsqrt(ms + EPS) * g_ref[...]


def _final_norm(x, g):
    bsz, s, d = x.shape
    tm = FFN_TM
    assert s % tm == 0
    return pl.pallas_call(
        _final_norm_kernel,
        grid=(bsz, s // tm),
        in_specs=[pl.BlockSpec((None, tm, d), lambda b, i: (b, i, 0)),
                  _const_spec(g.shape)],
        out_specs=pl.BlockSpec((None, tm, d), lambda b, i: (b, i, 0)),
        out_shape=jax.ShapeDtypeStruct(x.shape, F32),
        compiler_params=pltpu.CompilerParams(
            dimension_semantics=("arbitrary", "arbitrary"),
            vmem_limit_bytes=VMEM_LIMIT),
        name="final_norm",
    )(x, g)


def _block_diag(w):
    depth, n, a, b = w.shape
    eye = jnp.eye(n, dtype=w.dtype)
    return jnp.einsum('lnde,nm->lndme', w, eye).reshape(depth, n * a, n * b)


def kernel(x, c, ada_w, ada_b, norm_mix_g, norm_ffn_g, w_in, rg_conv_w, rg_conv_b, rg_wa, rg_ba, rg_wx, rg_bx, rg_lambda, gla_wg2, gla_bg, gla_norm_g, w_out, ffn_w1, ffn_w3, ffn_w2, router_w, moe_w1, moe_w3, moe_w2, final_g):
    bsz, s, d = x.shape
    depth = ada_w.shape[0]
    d_rg = rg_conv_w.shape[2]
    rank = gla_wg2.shape[1]
    n_exp = router_w.shape[2]
    d_main = w_in.shape[2] - rank

    mod = _adaln_mod(c, ada_w, ada_b).reshape(depth, bsz, 6, d)

    win_p = jnp.concatenate(
        [w_in[:, :, :d_main],
         jnp.pad(w_in[:, :, d_main:], ((0, 0), (0, 0), (0, LANES - rank)))], axis=2).astype(BF16)
    wg2_p = jnp.pad(gla_wg2, ((0, 0), (0, LANES - rank), (0, 0))).astype(BF16)
    wgate = jnp.concatenate([_block_diag(rg_wa), _block_diag(rg_wx)], axis=2).astype(BF16)
    bgate = jnp.concatenate([rg_ba, rg_bx], axis=1)[:, None, :]
    mixer_consts = [norm_mix_g[:, None, :], win_p, rg_conv_w, rg_conv_b[:, None, :], wgate, bgate,
                    rg_lambda[:, None, :], wg2_p, gla_bg[:, None, :], gla_norm_g[:, None, :],
                    w_out.astype(BF16)]
    ng_ffn = norm_ffn_g[:, None, :]
    ffn_w = [w.astype(BF16) for w in (ffn_w1, ffn_w3, ffn_w2)]
    moe_w = [w.astype(BF16) for w in (moe_w1, moe_w3, moe_w2)]
    rw_p = jnp.pad(router_w, ((0, 0), (0, 0), (0, LANES - n_exp)))

    for layer in range(depth):
        x = _mixer(x, mod[layer], layer, mixer_consts)
        j = layer // 2
        if layer % 2 == 0:
            x = _ffn(x, mod[layer], ng_ffn, layer, *ffn_w, j)
        else:
            last = layer == depth - 1
            x = _moe(x, mod[layer], ng_ffn, layer, rw_p, *moe_w, j, final_g[None, :],
                     final_norm=last)
            if last:
                return x
    return _final_norm(x, final_g[None, :])
```

```python
import functools

import jax
import jax.numpy as jnp
from jax import lax
from jax.experimental import pallas as pl
from jax.experimental.pallas import tpu as pltpu

F32 = jnp.float32
BF16 = jnp.bfloat16

EPS = 1e-6
RG_C = 8.0
CONV_W = 4
RG_BLOCKS = 8
GLA_HEADS = 4
GLA_TAU = 16.0
GLA_CHUNK = 64
TOP_K = 2

LANES = 128
SUBLANES = 8
VMEM_LIMIT = 56 * 1024 * 1024

MIX_TS = 256
MIX_NSEQ = 2
FFN_TM = 512
FFN_FC = 256
ROUTER_TR = 256
MOE_TM = 512


def _sigmoid(x):
    return 1.0 / (1.0 + jnp.exp(-x))


def _rms_mod(x, g, shift, scale):
    ms = jnp.mean(x * x, axis=-1, keepdims=True)
    return (x * lax.rsqrt(ms + EPS)) * (g * (1.0 + scale)) + shift


def _log2(n):
    assert n > 0 and n & (n - 1) == 0, n
    return n.bit_length() - 1


def _idiv(x, n):
    return lax.shift_right_logical(x, _log2(n))


def _imod(x, n):
    _log2(n)
    return x & (n - 1)


def _shift_rows(x, d, fill, seg=None):
    rolled = pltpu.roll(x, d, axis=0)
    row = lax.broadcasted_iota(jnp.int32, x.shape, 0)
    if seg is not None:
        row = _imod(row, seg)
    return jnp.where(row >= d, rolled, fill)


def _linear_scan(a, b, h0):
    rows, n = a.shape
    groups = rows // SUBLANES
    a3 = a.reshape(groups, SUBLANES, n)
    b3 = b.reshape(groups, SUBLANES, n)
    sub = lax.broadcasted_iota(jnp.int32, a3.shape, 1)
    d = 1
    while d < SUBLANES:
        keep = sub >= d
        b3 = a3 * jnp.where(keep, pltpu.roll(b3, d, axis=1), 0.0) + b3
        a3 = a3 * jnp.where(keep, pltpu.roll(a3, d, axis=1), 1.0)
        d *= 2
    carry = h0
    out = []
    for gi in range(groups):
        hg = a3[gi] * carry + b3[gi]
        out.append(hg)
        carry = hg[SUBLANES - 1:SUBLANES]
    return jnp.concatenate(out, axis=0)


def _mod_kernel(c_ref, w_ref, b_ref, o_ref):
    c = c_ref[...]
    ca = c * _sigmoid(c)
    o_ref[...] = jnp.dot(ca, w_ref[...], precision=lax.Precision.HIGHEST,
                         preferred_element_type=F32) + b_ref[...]


def _adaln_mod(c, ada_w, ada_b):
    depth, d, n = ada_w.shape
    bsz = c.shape[0]
    rows = -(-bsz // SUBLANES) * SUBLANES
    c_pad = jnp.pad(c, ((0, rows - bsz), (0, 0)))
    tn = 1536
    assert n % tn == 0
    out = pl.pallas_call(
        _mod_kernel,
        grid=(depth, n // tn),
        in_specs=[
            pl.BlockSpec((rows, d), lambda l, j: (0, 0)),
            pl.BlockSpec((None, d, tn), lambda l, j: (l, 0, j)),
            pl.BlockSpec((None, 1, tn), lambda l, j: (l, 0, j)),
        ],
        out_specs=pl.BlockSpec((None, rows, tn), lambda l, j: (l, 0, j)),
        out_shape=jax.ShapeDtypeStruct((depth, rows, n), F32),
        compiler_params=pltpu.CompilerParams(
            dimension_semantics=("arbitrary", "arbitrary"),
            vmem_limit_bytes=VMEM_LIMIT),
        name="adaln_mod",
    )(c_pad, ada_w, ada_b.reshape(depth, 1, n))
    return out[:, :bsz, :]


def _mixer_kernel(x_ref, mod_ref, *rest, **dims):
    consts, (o_ref, tail_ref, hst_ref, gst_ref) = rest[:-4], rest[-4:]
    s_idx = pl.program_id(1)

    @pl.when(s_idx == 0)
    def _():
        tail_ref[...] = jnp.zeros_like(tail_ref)
        hst_ref[...] = jnp.zeros_like(hst_ref)
        gst_ref[...] = jnp.zeros_like(gst_ref)

    for k in range(x_ref.shape[0]):
        _mixer_seq(x_ref.at[k], mod_ref.at[k], *consts, o_ref.at[k], tail_ref.at[k],
                   hst_ref.at[k], gst_ref.at[k], s_idx, **dims)


def _mixer_seq(x_ref, mod_ref, ng_ref, win_ref, cw_ref, cb_ref, wg_ref, bgate_ref,
               lam_ref, wg2_ref, bg_ref, gg_ref, wout_ref, o_ref,
               tail_ref, hst_ref, gst_ref, s_idx, *, d_rg, kd, vd):
    ts = x_ref.shape[0]
    dk = kd // GLA_HEADS
    dv = vd // GLA_HEADS

    x = x_ref[...]
    mod = mod_ref[...]
    h = _rms_mod(x, ng_ref[...], mod[0:1], mod[1:2]).astype(BF16)
    z = jnp.dot(h, win_ref[...], preferred_element_type=F32)

    o0 = 0
    rg_x = z[:, o0:o0 + d_rg]; o0 += d_rg
    rg_gate = z[:, o0:o0 + d_rg]; o0 += d_rg
    q = z[:, o0:o0 + kd]; o0 += kd
    k = z[:, o0:o0 + kd]; o0 += kd
    v = z[:, o0:o0 + vd]; o0 += vd
    g = z[:, o0:o0 + vd]; o0 += vd
    f_low = z[:, o0:o0 + LANES]

    cat = jnp.concatenate([tail_ref[...], rg_x], axis=0)
    cw = cw_ref[...]
    u = cb_ref[...] + cw[CONV_W - 1:CONV_W] * rg_x
    for tap in range(CONV_W - 1):
        sh = CONV_W - 1 - tap
        u = u + cw[tap:tap + 1] * pltpu.roll(cat, sh, axis=0)[SUBLANES:SUBLANES + ts]
    tail_ref[...] = rg_x[ts - SUBLANES:ts]

    gates = jnp.dot(u.astype(BF16), wg_ref[...], preferred_element_type=F32) + bgate_ref[...]
    r = _sigmoid(gates[:, :d_rg])
    i_gate = _sigmoid(gates[:, d_rg:])
    lam = lam_ref[...]
    sp = jnp.maximum(-lam, 0.0) + jnp.log1p(jnp.exp(-jnp.abs(lam)))
    log_a = -RG_C * r * sp
    a = jnp.exp(log_a)
    mult = jnp.sqrt(-jnp.tanh(log_a) * (a * a + 1.0))
    row = lax.broadcasted_iota(jnp.int32, (ts, d_rg), 0)
    mult = jnp.where((row == 0) & (s_idx == 0), 1.0, mult)
    b_term = mult * i_gate * u
    h_rg = _linear_scan(a, b_term, hst_ref[0:1])
    hst_ref[...] = jnp.broadcast_to(h_rg[ts - 1:ts], hst_ref.shape)
    gelu = 0.5 * rg_gate * (1.0 + jnp.tanh(0.7978845608028654 * (rg_gate + 0.044715 * rg_gate ** 3)))
    rg_out = h_rg * gelu

    fz = jnp.dot(f_low.astype(BF16), wg2_ref[...], preferred_element_type=F32) + bg_ref[...]
    log_f = (jnp.minimum(fz, 0.0) - jnp.log1p(jnp.exp(-jnp.abs(fz)))) * (1.0 / GLA_TAU)
    bcum = log_f
    d = 1
    while d < GLA_CHUNK:
        bcum = bcum + _shift_rows(bcum, d, 0.0, seg=GLA_CHUNK)
        d *= 2

    scale = dk ** -0.5
    nh = GLA_HEADS
    c_len = GLA_CHUNK
    r_i = lax.broadcasted_iota(jnp.int32, (nh * c_len, kd), 0)
    c_i = lax.broadcasted_iota(jnp.int32, (nh * c_len, kd), 1)
    head_mask = _idiv(r_i, c_len) == _idiv(c_i, dk)
    r_j = lax.broadcasted_iota(jnp.int32, (nh * c_len, c_len), 0)
    c_j = lax.broadcasted_iota(jnp.int32, (nh * c_len, c_len), 1)
    causal = c_j <= _imod(r_j, c_len)
    r_s = lax.broadcasted_iota(jnp.int32, (vd, kd), 0)
    c_s = lax.broadcasted_iota(jnp.int32, (vd, kd), 1)
    bd_mask = _idiv(r_s, dv) == _idiv(c_s, dk)

    state = gst_ref[...]
    o_chunks = []
    for ci in range(ts // c_len):
        lo = ci * c_len
        bc = bcum[lo:lo + c_len]
        qc = q[lo:lo + c_len] * scale
        kc = k[lo:lo + c_len]
        vc = v[lo:lo + c_len].astype(BF16)
        b_ref_row = bc[c_len // 2 - 1:c_len // 2]
        b_last = bc[c_len - 1:c_len]
        q_loc = (qc * jnp.exp(bc - b_ref_row)).astype(BF16)
        k_loc = (kc * jnp.exp(b_ref_row - bc)).astype(BF16)
        q_stack = jnp.where(head_mask, jnp.concatenate([q_loc] * nh, axis=0), jnp.zeros((), BF16))
        scores = lax.dot_general(q_stack, k_loc, (((1,), (1,)), ((), ())),
                                 preferred_element_type=F32)
        p = jnp.where(causal, scores, 0.0).astype(BF16)
        oi = jnp.dot(p, vc, preferred_element_type=F32)
        o_intra = jnp.concatenate(
            [oi[hh * c_len:(hh + 1) * c_len, hh * dv:(hh + 1) * dv] for hh in range(nh)], axis=1)
        q_b = (qc * jnp.exp(bc)).astype(BF16)
        o_inter = lax.dot_general(q_b, state.astype(BF16), (((1,), (1,)), ((), ())),
                                  preferred_element_type=F32)
        o_chunks.append(o_intra + o_inter)
        k_end = (kc * jnp.exp(b_last - bc)).astype(BF16)
        upd = lax.dot_general(vc, k_end, (((0,), (0,)), ((), ())),
                              preferred_element_type=F32)
        state = jnp.exp(b_last) * state + jnp.where(bd_mask, upd, 0.0)
    gst_ref[...] = state
    o = jnp.concatenate(o_chunks, axis=0)

    gg = gg_ref[...]
    heads = []
    for hh in range(nh):
        oh = o[:, hh * dv:(hh + 1) * dv]
        ms = jnp.mean(oh * oh, axis=-1, keepdims=True)
        heads.append(oh * lax.rsqrt(ms + EPS) * gg)
    gla_out = jnp.concatenate(heads, axis=1) * (g * _sigmoid(g))

    mix_in = jnp.concatenate([rg_out, gla_out], axis=1).astype(BF16)
    mix = jnp.dot(mix_in, wout_ref[...], preferred_element_type=F32)
    o_ref[...] = x + mod[2:3] * mix


def _const_spec(shape):
    nd = len(shape)
    return pl.BlockSpec(shape, lambda *_: (0,) * nd, pipeline_mode=pl.Buffered(1))


def _layer_spec(stacked, layer):
    rest = stacked.shape[1:]
    return pl.BlockSpec((None,) + rest, lambda *_: (layer,) + (0,) * len(rest),
                        pipeline_mode=pl.Buffered(1))


def _mixer(x, mod6, layer, consts):
    bsz, s, d = x.shape
    d_rg = consts[2].shape[2]
    kd = consts[8].shape[2]
    vd = consts[10].shape[1] - d_rg
    ts = MIX_TS
    nseq = MIX_NSEQ if bsz % MIX_NSEQ == 0 else 1
    assert s % ts == 0 and ts % GLA_CHUNK == 0
    kern = functools.partial(_mixer_kernel, d_rg=d_rg, kd=kd, vd=vd)
    return pl.pallas_call(
        kern,
        grid=(bsz // nseq, s // ts),
        in_specs=[
            pl.BlockSpec((nseq, ts, d), lambda b, i: (b, i, 0)),
            pl.BlockSpec((nseq, 6, d), lambda b, i: (b, 0, 0)),
        ] + [_layer_spec(a, layer) for a in consts],
        out_specs=pl.BlockSpec((nseq, ts, d), lambda b, i: (b, i, 0)),
        out_shape=jax.ShapeDtypeStruct(x.shape, F32),
        scratch_shapes=[
            pltpu.VMEM((nseq, SUBLANES, d_rg), F32),
            pltpu.VMEM((nseq, SUBLANES, d_rg), F32),
            pltpu.VMEM((nseq, vd, kd), F32),
        ],
        compiler_params=pltpu.CompilerParams(
            dimension_semantics=("arbitrary", "arbitrary"),
            vmem_limit_bytes=VMEM_LIMIT),
        name="mixer",
    )(x, mod6, *consts)


def _swiglu_acc(h, w1_ref, w3_ref, w2_ref, acc, between=None):
    f = w1_ref.shape[-1]
    n = f // FFN_FC
    side = between if between is not None else (lambda k, total: None)
    for j in range(n):
        sl = slice(j * FFN_FC, (j + 1) * FFN_FC)
        a = jnp.dot(h, w1_ref[:, sl], preferred_element_type=F32)
        side(3 * j, 3 * n)
        b = jnp.dot(h, w3_ref[:, sl], preferred_element_type=F32)
        side(3 * j + 1, 3 * n)
        p = (a * _sigmoid(a) * b).astype(BF16)
        acc = acc + jnp.dot(p, w2_ref[sl, :], preferred_element_type=F32)
        side(3 * j + 2, 3 * n)
    return acc


def _ffn_kernel(x_ref, mod_ref, ng_ref, w1_ref, w3_ref, w2_ref, o_ref):
    x = x_ref[...]
    mod = mod_ref[...]
    h = _rms_mod(x, ng_ref[...], mod[3:4], mod[4:5]).astype(BF16)
    acc = _swiglu_acc(h, w1_ref, w3_ref, w2_ref, jnp.zeros(x.shape, F32))
    o_ref[...] = x + mod[5:6] * acc


def _ffn(x, mod6, ng, layer, w1, w3, w2, j):
    bsz, s, d = x.shape
    f = w1.shape[2]
    tm = FFN_TM
    assert s % tm == 0 and f % FFN_FC == 0
    return pl.pallas_call(
        _ffn_kernel,
        grid=(bsz, s // tm),
        in_specs=[
            pl.BlockSpec((None, tm, d), lambda b, i: (b, i, 0)),
            pl.BlockSpec((None, 6, d), lambda b, i: (b, 0, 0)),
            _layer_spec(ng, layer), _layer_spec(w1, j), _layer_spec(w3, j), _layer_spec(w2, j),
        ],
        out_specs=pl.BlockSpec((None, tm, d), lambda b, i: (b, i, 0)),
        out_shape=jax.ShapeDtypeStruct(x.shape, F32),
        compiler_params=pltpu.CompilerParams(
            dimension_semantics=("arbitrary", "arbitrary"),
            vmem_limit_bytes=VMEM_LIMIT),
        name="ffn_dense",
    )(x, mod6, ng, w1, w3, w2)


def _top2(logits, n_exp):
    lane = lax.broadcasted_iota(jnp.int32, logits.shape, 1).astype(F32)
    neg = jnp.float32(-jnp.inf)
    lg = jnp.where(lane < n_exp, logits, neg)
    m1 = jnp.max(lg, axis=-1, keepdims=True)
    i1 = jnp.min(jnp.where(lg == m1, lane, float(LANES)), axis=-1, keepdims=True)
    lg2 = jnp.where(lane == i1, neg, lg)
    m2 = jnp.max(lg2, axis=-1, keepdims=True)
    i2 = jnp.min(jnp.where(lg2 == m2, lane, float(LANES)), axis=-1, keepdims=True)
    ex = jnp.exp(m2 - m1)
    return i1, i2, 1.0 / (1.0 + ex), ex / (1.0 + ex)


def _router_kernel(x_ref, mod_ref, ng_ref, rwhi_ref, rwlo_ref, h_ref, info_ref, cnt_ref, carry_ref,
                   *, n_exp):
    first = (pl.program_id(0) == 0) & (pl.program_id(1) == 0)

    @pl.when(first)
    def _():
        carry_ref[...] = jnp.zeros_like(carry_ref)

    x = x_ref[...]
    mod = mod_ref[...]
    h = _rms_mod(x, ng_ref[...], mod[3:4], mod[4:5])
    h_ref[...] = h
    h_hi = h.astype(BF16)
    h_lo = (h - h_hi.astype(F32)).astype(BF16)
    logits = (jnp.dot(h_hi, rwhi_ref[...], preferred_element_type=F32)
              + (jnp.dot(h_hi, rwlo_ref[...], preferred_element_type=F32)
                 + jnp.dot(h_lo, rwhi_ref[...], preferred_element_type=F32)))
    i1, i2, p1, p2 = _top2(logits, n_exp)
    tr = x.shape[0]
    lane = lax.broadcasted_iota(jnp.int32, (tr, LANES), 1).astype(F32)
    oh1 = lane == i1
    oh2 = lane == i2
    sel = jnp.where(oh1, 1.0, 0.0) + jnp.where(oh2, 1.0, 0.0)
    r_t = lax.broadcasted_iota(jnp.int32, (tr, tr), 0)
    c_t = lax.broadcasted_iota(jnp.int32, (tr, tr), 1)
    tri = jnp.where(r_t > c_t, 1.0, 0.0).astype(BF16)
    excl = jnp.dot(tri, sel.astype(BF16), preferred_element_type=F32)
    base = carry_ref[0:1] + excl
    rank1 = jnp.sum(jnp.where(oh1, base, 0.0), axis=-1, keepdims=True)
    rank2 = jnp.sum(jnp.where(oh2, base, 0.0), axis=-1, keepdims=True)
    info = jnp.where(lane == 0.0, i1, 0.0)
    for col, val in ((1.0, i2), (2.0, rank1), (3.0, rank2), (4.0, p1), (5.0, p2)):
        info = jnp.where(lane == col, val, info)
    info_ref[...] = info
    carry = carry_ref[...] + jnp.sum(sel, axis=0, keepdims=True)
    carry_ref[...] = carry
    cnt_ref[...] = carry


def _router(x, mod6, ng, layer, rw_p, j, n_exp):
    bsz, s, d = x.shape
    tr = ROUTER_TR
    assert s % tr == 0
    nblk = s // tr
    t = bsz * s
    kern = functools.partial(_router_kernel, n_exp=n_exp)
    return pl.pallas_call(
        kern,
        grid=(bsz, nblk),
        in_specs=[
            pl.BlockSpec((None, tr, d), lambda b, i: (b, i, 0)),
            pl.BlockSpec((None, 6, d), lambda b, i: (b, 0, 0)),
            _layer_spec(ng, layer), _layer_spec(rw_p[0], j), _layer_spec(rw_p[1], j),
        ],
        out_specs=[
            pl.BlockSpec((tr, d), lambda b, i: (b * nblk + i, 0)),
            pl.BlockSpec((tr, LANES), lambda b, i: (b * nblk + i, 0)),
            pl.BlockSpec((SUBLANES, LANES), lambda b, i: (0, 0)),
        ],
        out_shape=[
            jax.ShapeDtypeStruct((t, d), F32),
            jax.ShapeDtypeStruct((t, LANES), F32),
            jax.ShapeDtypeStruct((SUBLANES, LANES), F32),
        ],
        scratch_shapes=[pltpu.VMEM((SUBLANES, LANES), F32)],
        compiler_params=pltpu.CompilerParams(
            dimension_semantics=("arbitrary", "arbitrary"),
            vmem_limit_bytes=VMEM_LIMIT),
        name="moe_router",
    )(x, mod6, ng, *rw_p)


def _row_gather_start(src_hbm, row, dst, dst_row, sem):
    pltpu.make_async_copy(src_hbm.at[pl.ds(row, 1)], dst.at[pl.ds(dst_row, 1)], sem).start()


def _row_gather_wait(src_hbm, dst, sem):
    pltpu.make_async_copy(src_hbm.at[pl.ds(0, dst.shape[0])], dst, sem).wait()


def _expert_kernel(te_ref, nt_ref, cnt_ref, start_ref, d1_ref, d2_ref, h_hbm, w1_ref, w3_ref,
                   w2_ref, o_ref, xbuf, src_ref, sem):
    i = pl.program_id(0)
    n_tiles = nt_ref[0]
    tm = xbuf.shape[1]
    n_tok = d1_ref.shape[0]

    @pl.when(i == 0)
    def _():
        for e in range(cnt_ref.shape[0]):
            lo = start_ref[e] + cnt_ref[e]
            hi = start_ref[e] + _idiv(cnt_ref[e] + (tm - 1), tm) * tm

            def pad_body(p, carry):
                src_ref[p] = 0
                return carry

            lax.fori_loop(lo, hi, pad_body, 0)

        def inv_body(t, carry):
            src_ref[d1_ref[t]] = t
            src_ref[d2_ref[t]] = t
            return carry

        lax.fori_loop(0, n_tok, inv_body, 0, unroll=8)

        def first_body(r, carry):
            _row_gather_start(h_hbm, src_ref[r], xbuf.at[0], r, sem.at[0])
            return carry

        lax.fori_loop(0, tm, first_body, 0, unroll=8)

    @pl.when(i < n_tiles)
    def _():
        slot = i % 2
        nxt_slot = 1 - slot
        nxt_base = jnp.minimum(i + 1, n_tiles - 1) * tm
        _row_gather_wait(h_hbm, xbuf.at[slot], sem.at[slot])
        xs = xbuf[slot].astype(BF16)

        def issue_part(j, n):
            per = -(-tm // n)
            for r in range(j * per, min((j + 1) * per, tm)):
                _row_gather_start(h_hbm, src_ref[nxt_base + r], xbuf.at[nxt_slot], r,
                                  sem.at[nxt_slot])

        o_ref[...] = _swiglu_acc(xs, w1_ref, w3_ref, w2_ref, jnp.zeros(o_ref.shape, F32),
                                 between=issue_part)

        @pl.when(i + 1 >= n_tiles)
        def _():
            _row_gather_wait(h_hbm, xbuf.at[nxt_slot], sem.at[nxt_slot])

    @pl.when(i >= n_tiles)
    def _():
        o_ref[...] = jnp.zeros_like(o_ref)


def _experts(tile_expert, n_tiles, counts, row_start, dest1, dest2, h, w1, w3, w2, j, nt_max):
    t, d = h.shape
    f = w1.shape[3]
    tm = MOE_TM
    assert f % FFN_FC == 0

    def w_map(i, te, *_):
        return (j, te[i], 0, 0)

    grid_spec = pltpu.PrefetchScalarGridSpec(
        num_scalar_prefetch=6,
        grid=(nt_max,),
        in_specs=[
            pl.BlockSpec(memory_space=pl.ANY),
            pl.BlockSpec((None, None, d, f), w_map),
            pl.BlockSpec((None, None, d, f), w_map),
            pl.BlockSpec((None, None, f, d), w_map),
        ],
        out_specs=pl.BlockSpec((tm, d), lambda i, *_: (i, 0)),
        scratch_shapes=[
            pltpu.VMEM((2, tm, d), F32),
            pltpu.SMEM((nt_max * tm,), jnp.int32),
            pltpu.SemaphoreType.DMA((2,)),
        ],
    )
    return pl.pallas_call(
        _expert_kernel,
        grid_spec=grid_spec,
        out_shape=jax.ShapeDtypeStruct((nt_max * tm, d), F32),
        compiler_params=pltpu.CompilerParams(
            dimension_semantics=("arbitrary",),
            vmem_limit_bytes=VMEM_LIMIT),
        name="moe_experts",
    )(tile_expert, n_tiles, counts, row_start, dest1, dest2, h, w1, w3, w2)


def _combine_kernel(d1_ref, d2_ref, x_ref, mod_ref, info_ref, fg_ref, y_hbm, o_ref, ybuf, sem,
                    *, final_norm):
    i = pl.program_id(0)
    n = pl.num_programs(0)
    tc = x_ref.shape[0]

    def issue(tile, slot):
        base = tile * tc

        def body(r, carry):
            _row_gather_start(y_hbm, d1_ref[base + r], ybuf.at[slot, 0], r, sem.at[slot])
            _row_gather_start(y_hbm, d2_ref[base + r], ybuf.at[slot, 1], r, sem.at[slot])
            return carry

        lax.fori_loop(0, tc, body, 0, unroll=8)

    @pl.when(i == 0)
    def _():
        issue(0, 0)

    @pl.when(i + 1 < n)
    def _():
        issue(i + 1, (i + 1) % 2)

    slot = i % 2
    _row_gather_wait(y_hbm, ybuf.at[slot, 0], sem.at[slot])
    _row_gather_wait(y_hbm, ybuf.at[slot, 1], sem.at[slot])
    info = info_ref[...]
    y = info[:, 4:5] * ybuf[slot, 0] + info[:, 5:6] * ybuf[slot, 1]
    out = x_ref[...] + mod_ref[5:6, :] * y
    if final_norm:
        ms = jnp.mean(out * out, axis=-1, keepdims=True)
        out = out * lax.rsqrt(ms + EPS) * fg_ref[...]
    o_ref[...] = out


def _combine(dest1, dest2, x, mod6, info, final_g, y, final_norm):
    bsz, s, d = x.shape
    tc = ROUTER_TR
    nblk = s // tc
    kern = functools.partial(_combine_kernel, final_norm=final_norm)
    grid_spec = pltpu.PrefetchScalarGridSpec(
        num_scalar_prefetch=2,
        grid=(bsz * nblk,),
        in_specs=[
            pl.BlockSpec((None, tc, d), lambda i, d1, d2: (i // nblk, i % nblk, 0)),
            pl.BlockSpec((None, 6, d), lambda i, d1, d2: (i // nblk, 0, 0)),
            pl.BlockSpec((tc, LANES), lambda i, d1, d2: (i, 0)),
            pl.BlockSpec(final_g.shape, lambda i, d1, d2: (0, 0)),
            pl.BlockSpec(memory_space=pl.ANY),
        ],
        out_specs=pl.BlockSpec((None, tc, d), lambda i, d1, d2: (i // nblk, i % nblk, 0)),
        scratch_shapes=[
            pltpu.VMEM((2, 2, tc, d), F32),
            pltpu.SemaphoreType.DMA((2,)),
        ],
    )
    return pl.pallas_call(
        kern,
        grid_spec=grid_spec,
        out_shape=jax.ShapeDtypeStruct(x.shape, F32),
        compiler_params=pltpu.CompilerParams(
            dimension_semantics=("arbitrary",),
            vmem_limit_bytes=VMEM_LIMIT),
        name="moe_combine",
    )(dest1, dest2, x, mod6, info, final_g, y)


def _moe(x, mod6, ng, layer, rw_p, w1, w3, w2, j, final_g, final_norm):
    bsz, s, d = x.shape
    n_exp = w1.shape[1]
    t = bsz * s
    tm = MOE_TM
    nt_max = (TOP_K * t) // tm + n_exp
    h, info, cnt = _router(x, mod6, ng, layer, rw_p, j, n_exp)

    counts = cnt[0, :n_exp].astype(jnp.int32)
    tiles_per = (counts + tm - 1) // tm
    tile_end = jnp.cumsum(tiles_per)
    row_start = (tile_end - tiles_per) * tm
    n_tiles = tile_end[-1]
    e1 = info[:, 0].astype(jnp.int32)
    e2 = info[:, 1].astype(jnp.int32)
    dest1 = row_start[e1] + info[:, 2].astype(jnp.int32)
    dest2 = row_start[e2] + info[:, 3].astype(jnp.int32)
    tile_ids = jnp.arange(nt_max, dtype=jnp.int32)
    live_ids = jnp.minimum(tile_ids, n_tiles - 1)
    tile_expert = jnp.sum((live_ids[:, None] >= tile_end[None, :]).astype(jnp.int32), axis=1)
    tile_expert = jnp.minimum(tile_expert, n_exp - 1)

    y = _experts(tile_expert, n_tiles[None].astype(jnp.int32), counts, row_start, dest1, dest2,
                 h, w1, w3, w2, j, nt_max)
    return _combine(dest1, dest2, x, mod6, info, final_g, y, final_norm)


def _final_norm_kernel(x_ref, g_ref, o_ref):
    x = x_ref[...]
    ms = jnp.mean(x * x, axis=-1, keepdims=True)
    o_ref[...] = x * lax.rsqrt(ms + EPS) * g_ref[...]


def _final_norm(x, g):
    bsz, s, d = x.shape
    tm = FFN_TM
    assert s % tm == 0
    return pl.pallas_call(
        _final_norm_kernel,
        grid=(bsz, s // tm),
        in_specs=[pl.BlockSpec((None, tm, d), lambda b, i: (b, i, 0)),
                  _const_spec(g.shape)],
        out_specs=pl.BlockSpec((None, tm, d), lambda b, i: (b, i, 0)),
        out_shape=jax.ShapeDtypeStruct(x.shape, F32),
        compiler_params=pltpu.CompilerParams(
            dimension_semantics=("arbitrary", "arbitrary"),
            vmem_limit_bytes=VMEM_LIMIT),
        name="final_norm",
    )(x, g)


def _block_diag(w):
    depth, n, a, b = w.shape
    eye = jnp.eye(n, dtype=w.dtype)
    return jnp.einsum('lnde,nm->lndme', w, eye).reshape(depth, n * a, n * b)


def kernel(x, c, ada_w, ada_b, norm_mix_g, norm_ffn_g, w_in, rg_conv_w, rg_conv_b, rg_wa, rg_ba, rg_wx, rg_bx, rg_lambda, gla_wg2, gla_bg, gla_norm_g, w_out, ffn_w1, ffn_w3, ffn_w2, router_w, moe_w1, moe_w3, moe_w2, final_g):
    bsz, s, d = x.shape
    depth = ada_w.shape[0]
    d_rg = rg_conv_w.shape[2]
    rank = gla_wg2.shape[1]
    n_exp = router_w.shape[2]
    d_main = w_in.shape[2] - rank

    mod = _adaln_mod(c, ada_w, ada_b).reshape(depth, bsz, 6, d)

    win_p = jnp.concatenate(
        [w_in[:, :, :d_main],
         jnp.pad(w_in[:, :, d_main:], ((0, 0), (0, 0), (0, LANES - rank)))], axis=2).astype(BF16)
    wg2_p = jnp.pad(gla_wg2, ((0, 0), (0, LANES - rank), (0, 0))).astype(BF16)
    wgate = jnp.concatenate([_block_diag(rg_wa), _block_diag(rg_wx)], axis=2).astype(BF16)
    bgate = jnp.concatenate([rg_ba, rg_bx], axis=1)[:, None, :]
    mixer_consts = [norm_mix_g[:, None, :], win_p, rg_conv_w, rg_conv_b[:, None, :], wgate, bgate,
                    rg_lambda[:, None, :], wg2_p, gla_bg[:, None, :], gla_norm_g[:, None, :],
                    w_out.astype(BF16)]
    ng_ffn = norm_ffn_g[:, None, :]
    ffn_w = [w.astype(BF16) for w in (ffn_w1, ffn_w3, ffn_w2)]
    moe_w = [w.astype(BF16) for w in (moe_w1, moe_w3, moe_w2)]
    rw_f32 = jnp.pad(router_w, ((0, 0), (0, 0), (0, LANES - n_exp)))
    rw_hi = rw_f32.astype(BF16)
    rw_p = (rw_hi, (rw_f32 - rw_hi.astype(F32)).astype(BF16))

    for layer in range(depth):
        x = _mixer(x, mod[layer], layer, mixer_consts)
        j = layer // 2
        if layer % 2 == 0:
            x = _ffn(x, mod[layer], ng_ffn, layer, *ffn_w, j)
        else:
            last = layer == depth - 1
            x = _moe(x, mod[layer], ng_ffn, layer, rw_p, *moe_w, j, final_g[None, :],
                     final_norm=last)
            if last:
                return x
    return _final_norm(x, final_g[None, :])
```

```python
import functools

import jax
import jax.numpy as jnp
from jax import lax
from jax.experimental import pallas as pl
from jax.experimental.pallas import tpu as pltpu

F32 = jnp.float32
BF16 = jnp.bfloat16

EPS = 1e-6
RG_C = 8.0
CONV_W = 4
RG_BLOCKS = 8
GLA_HEADS = 4
GLA_TAU = 16.0
GLA_CHUNK = 64
TOP_K = 2

LANES = 128
SUBLANES = 8
VMEM_LIMIT = 56 * 1024 * 1024

MIX_TS = 256
MIX_NSEQ = 2
FFN_TM = 512
FFN_FC = 256
ROUTER_TR = 256
MOE_TM = 512


def _sigmoid(x):
    return 1.0 / (1.0 + jnp.exp(-x))


def _rms_mod(x, g, shift, scale):
    ms = jnp.mean(x * x, axis=-1, keepdims=True)
    return (x * lax.rsqrt(ms + EPS)) * (g * (1.0 + scale)) + shift


def _log2(n):
    assert n > 0 and n & (n - 1) == 0, n
    return n.bit_length() - 1


def _idiv(x, n):
    return lax.shift_right_logical(x, _log2(n))


def _imod(x, n):
    _log2(n)
    return x & (n - 1)


def _shift_rows(x, d, fill, seg=None):
    rolled = pltpu.roll(x, d, axis=0)
    row = lax.broadcasted_iota(jnp.int32, x.shape, 0)
    if seg is not None:
        row = _imod(row, seg)
    return jnp.where(row >= d, rolled, fill)


def _linear_scan(a, b, h0):
    rows, n = a.shape
    groups = rows // SUBLANES
    a3 = a.reshape(groups, SUBLANES, n)
    b3 = b.reshape(groups, SUBLANES, n)
    sub = lax.broadcasted_iota(jnp.int32, a3.shape, 1)
    d = 1
    while d < SUBLANES:
        keep = sub >= d
        b3 = a3 * jnp.where(keep, pltpu.roll(b3, d, axis=1), 0.0) + b3
        a3 = a3 * jnp.where(keep, pltpu.roll(a3, d, axis=1), 1.0)
        d *= 2
    carry = h0
    out = []
    for gi in range(groups):
        hg = a3[gi] * carry + b3[gi]
        out.append(hg)
        carry = hg[SUBLANES - 1:SUBLANES]
    return jnp.concatenate(out, axis=0)


def _mod_kernel(c_ref, w_ref, b_ref, o_ref):
    c = c_ref[...]
    ca = c * _sigmoid(c)
    o_ref[...] = jnp.dot(ca, w_ref[...], precision=lax.Precision.HIGHEST,
                         preferred_element_type=F32) + b_ref[...]


def _adaln_mod(c, ada_w, ada_b):
    depth, d, n = ada_w.shape
    bsz = c.shape[0]
    rows = -(-bsz // SUBLANES) * SUBLANES
    c_pad = jnp.pad(c, ((0, rows - bsz), (0, 0)))
    tn = 1536
    assert n % tn == 0
    out = pl.pallas_call(
        _mod_kernel,
        grid=(depth, n // tn),
        in_specs=[
            pl.BlockSpec((rows, d), lambda l, j: (0, 0)),
            pl.BlockSpec((None, d, tn), lambda l, j: (l, 0, j)),
            pl.BlockSpec((None, 1, tn), lambda l, j: (l, 0, j)),
        ],
        out_specs=pl.BlockSpec((None, rows, tn), lambda l, j: (l, 0, j)),
        out_shape=jax.ShapeDtypeStruct((depth, rows, n), F32),
        compiler_params=pltpu.CompilerParams(
            dimension_semantics=("arbitrary", "arbitrary"),
            vmem_limit_bytes=VMEM_LIMIT),
        name="adaln_mod",
    )(c_pad, ada_w, ada_b.reshape(depth, 1, n))
    return out[:, :bsz, :]


def _mixer_kernel(x_ref, mod_ref, *rest, **dims):
    consts, (o_ref, tail_ref, hst_ref, gst_ref) = rest[:-4], rest[-4:]
    s_idx = pl.program_id(1)

    @pl.when(s_idx == 0)
    def _():
        tail_ref[...] = jnp.zeros_like(tail_ref)
        hst_ref[...] = jnp.zeros_like(hst_ref)
        gst_ref[...] = jnp.zeros_like(gst_ref)

    for k in range(x_ref.shape[0]):
        _mixer_seq(x_ref.at[k], mod_ref.at[k], *consts, o_ref.at[k], tail_ref.at[k],
                   hst_ref.at[k], gst_ref.at[k], s_idx, **dims)


def _mixer_seq(x_ref, mod_ref, ng_ref, win_ref, cw_ref, cb_ref, wg_ref, bgate_ref,
               lam_ref, wg2_ref, bg_ref, gg_ref, wout_ref, o_ref,
               tail_ref, hst_ref, gst_ref, s_idx, *, d_rg, kd, vd):
    ts = x_ref.shape[0]
    dk = kd // GLA_HEADS
    dv = vd // GLA_HEADS

    x = x_ref[...]
    mod = mod_ref[...]
    h = _rms_mod(x, ng_ref[...], mod[0:1], mod[1:2]).astype(BF16)
    z = jnp.dot(h, win_ref[...], preferred_element_type=F32)

    o0 = 0
    rg_x = z[:, o0:o0 + d_rg]; o0 += d_rg
    rg_gate = z[:, o0:o0 + d_rg]; o0 += d_rg
    q = z[:, o0:o0 + kd]; o0 += kd
    k = z[:, o0:o0 + kd]; o0 += kd
    v = z[:, o0:o0 + vd]; o0 += vd
    g = z[:, o0:o0 + vd]; o0 += vd
    f_low = z[:, o0:o0 + LANES]

    cat = jnp.concatenate([tail_ref[...], rg_x], axis=0)
    cw = cw_ref[...]
    u = cb_ref[...] + cw[CONV_W - 1:CONV_W] * rg_x
    for tap in range(CONV_W - 1):
        sh = CONV_W - 1 - tap
        u = u + cw[tap:tap + 1] * pltpu.roll(cat, sh, axis=0)[SUBLANES:SUBLANES + ts]
    tail_ref[...] = rg_x[ts - SUBLANES:ts]

    gates = jnp.dot(u.astype(BF16), wg_ref[...], preferred_element_type=F32) + bgate_ref[...]
    r = _sigmoid(gates[:, :d_rg])
    i_gate = _sigmoid(gates[:, d_rg:])
    lam = lam_ref[...]
    sp = jnp.maximum(-lam, 0.0) + jnp.log1p(jnp.exp(-jnp.abs(lam)))
    log_a = -RG_C * r * sp
    a = jnp.exp(log_a)
    mult = jnp.sqrt(-jnp.tanh(log_a) * (a * a + 1.0))
    row = lax.broadcasted_iota(jnp.int32, (ts, d_rg), 0)
    mult = jnp.where((row == 0) & (s_idx == 0), 1.0, mult)
    b_term = mult * i_gate * u
    h_rg = _linear_scan(a, b_term, hst_ref[0:1])
    hst_ref[...] = jnp.broadcast_to(h_rg[ts - 1:ts], hst_ref.shape)
    gelu = 0.5 * rg_gate * (1.0 + jnp.tanh(0.7978845608028654 * (rg_gate + 0.044715 * rg_gate ** 3)))
    rg_out = h_rg * gelu

    fz = jnp.dot(f_low.astype(BF16), wg2_ref[...], preferred_element_type=F32) + bg_ref[...]
    log_f = (jnp.minimum(fz, 0.0) - jnp.log1p(jnp.exp(-jnp.abs(fz)))) * (1.0 / GLA_TAU)
    bcum = log_f
    d = 1
    while d < GLA_CHUNK:
        bcum = bcum + _shift_rows(bcum, d, 0.0, seg=GLA_CHUNK)
        d *= 2

    scale = dk ** -0.5
    nh = GLA_HEADS
    c_len = GLA_CHUNK
    r_i = lax.broadcasted_iota(jnp.int32, (nh * c_len, kd), 0)
    c_i = lax.broadcasted_iota(jnp.int32, (nh * c_len, kd), 1)
    head_mask = _idiv(r_i, c_len) == _idiv(c_i, dk)
    r_j = lax.broadcasted_iota(jnp.int32, (nh * c_len, c_len), 0)
    c_j = lax.broadcasted_iota(jnp.int32, (nh * c_len, c_len), 1)
    causal = c_j <= _imod(r_j, c_len)
    r_s = lax.broadcasted_iota(jnp.int32, (vd, kd), 0)
    c_s = lax.broadcasted_iota(jnp.int32, (vd, kd), 1)
    bd_mask = _idiv(r_s, dv) == _idiv(c_s, dk)

    state = gst_ref[...]
    o_chunks = []
    for ci in range(ts // c_len):
        lo = ci * c_len
        bc = bcum[lo:lo + c_len]
        qc = q[lo:lo + c_len] * scale
        kc = k[lo:lo + c_len]
        vc = v[lo:lo + c_len].astype(BF16)
        b_ref_row = bc[c_len // 2 - 1:c_len // 2]
        b_last = bc[c_len - 1:c_len]
        q_loc = (qc * jnp.exp(bc - b_ref_row)).astype(BF16)
        k_loc = (kc * jnp.exp(b_ref_row - bc)).astype(BF16)
        q_stack = jnp.where(head_mask, jnp.concatenate([q_loc] * nh, axis=0), jnp.zeros((), BF16))
        scores = lax.dot_general(q_stack, k_loc, (((1,), (1,)), ((), ())),
                                 preferred_element_type=F32)
        p = jnp.where(causal, scores, 0.0).astype(BF16)
        oi = jnp.dot(p, vc, preferred_element_type=F32)
        o_intra = jnp.concatenate(
            [oi[hh * c_len:(hh + 1) * c_len, hh * dv:(hh + 1) * dv] for hh in range(nh)], axis=1)
        q_b = (qc * jnp.exp(bc)).astype(BF16)
        o_inter = lax.dot_general(q_b, state.astype(BF16), (((1,), (1,)), ((), ())),
                                  preferred_element_type=F32)
        o_chunks.append(o_intra + o_inter)
        k_end = (kc * jnp.exp(b_last - bc)).astype(BF16)
        upd = lax.dot_general(vc, k_end, (((0,), (0,)), ((), ())),
                              preferred_element_type=F32)
        state = jnp.exp(b_last) * state + jnp.where(bd_mask, upd, 0.0)
    gst_ref[...] = state
    o = jnp.concatenate(o_chunks, axis=0)

    gg = gg_ref[...]
    heads = []
    for hh in range(nh):
        oh = o[:, hh * dv:(hh + 1) * dv]
        ms = jnp.mean(oh * oh, axis=-1, keepdims=True)
        heads.append(oh * lax.rsqrt(ms + EPS) * gg)
    gla_out = jnp.concatenate(heads, axis=1) * (g * _sigmoid(g))

    mix_in = jnp.concatenate([rg_out, gla_out], axis=1).astype(BF16)
    mix = jnp.dot(mix_in, wout_ref[...], preferred_element_type=F32)
    o_ref[...] = x + mod[2:3] * mix


def _const_spec(shape):
    nd = len(shape)
    return pl.BlockSpec(shape, lambda *_: (0,) * nd, pipeline_mode=pl.Buffered(1))


def _layer_spec(stacked, layer):
    rest = stacked.shape[1:]
    return pl.BlockSpec((None,) + rest, lambda *_: (layer,) + (0,) * len(rest),
                        pipeline_mode=pl.Buffered(1))


def _mixer(x, mod6, layer, consts):
    bsz, s, d = x.shape
    d_rg = consts[2].shape[2]
    kd = consts[8].shape[2]
    vd = consts[10].shape[1] - d_rg
    ts = MIX_TS
    nseq = MIX_NSEQ if bsz % MIX_NSEQ == 0 else 1
    assert s % ts == 0 and ts % GLA_CHUNK == 0
    kern = functools.partial(_mixer_kernel, d_rg=d_rg, kd=kd, vd=vd)
    return pl.pallas_call(
        kern,
        grid=(bsz // nseq, s // ts),
        in_specs=[
            pl.BlockSpec((nseq, ts, d), lambda b, i: (b, i, 0)),
            pl.BlockSpec((nseq, 6, d), lambda b, i: (b, 0, 0)),
        ] + [_layer_spec(a, layer) for a in consts],
        out_specs=pl.BlockSpec((nseq, ts, d), lambda b, i: (b, i, 0)),
        out_shape=jax.ShapeDtypeStruct(x.shape, F32),
        scratch_shapes=[
            pltpu.VMEM((nseq, SUBLANES, d_rg), F32),
            pltpu.VMEM((nseq, SUBLANES, d_rg), F32),
            pltpu.VMEM((nseq, vd, kd), F32),
        ],
        compiler_params=pltpu.CompilerParams(
            dimension_semantics=("arbitrary", "arbitrary"),
            vmem_limit_bytes=VMEM_LIMIT),
        name="mixer",
    )(x, mod6, *consts)


def _swiglu_acc(h, w1_ref, w3_ref, w2_ref, acc):
    f = w1_ref.shape[-1]
    for j in range(f // FFN_FC):
        sl = slice(j * FFN_FC, (j + 1) * FFN_FC)
        a = jnp.dot(h, w1_ref[:, sl], preferred_element_type=F32)
        b = jnp.dot(h, w3_ref[:, sl], preferred_element_type=F32)
        p = (a * _sigmoid(a) * b).astype(BF16)
        acc = acc + jnp.dot(p, w2_ref[sl, :], preferred_element_type=F32)
    return acc


def _ffn_kernel(x_ref, mod_ref, ng_ref, w1_ref, w3_ref, w2_ref, o_ref):
    x = x_ref[...]
    mod = mod_ref[...]
    h = _rms_mod(x, ng_ref[...], mod[3:4], mod[4:5]).astype(BF16)
    acc = _swiglu_acc(h, w1_ref, w3_ref, w2_ref, jnp.zeros(x.shape, F32))
    o_ref[...] = x + mod[5:6] * acc


def _ffn(x, mod6, ng, layer, w1, w3, w2, j):
    bsz, s, d = x.shape
    f = w1.shape[2]
    tm = FFN_TM
    assert s % tm == 0 and f % FFN_FC == 0
    return pl.pallas_call(
        _ffn_kernel,
        grid=(bsz, s // tm),
        in_specs=[
            pl.BlockSpec((None, tm, d), lambda b, i: (b, i, 0)),
            pl.BlockSpec((None, 6, d), lambda b, i: (b, 0, 0)),
            _layer_spec(ng, layer), _layer_spec(w1, j), _layer_spec(w3, j), _layer_spec(w2, j),
        ],
        out_specs=pl.BlockSpec((None, tm, d), lambda b, i: (b, i, 0)),
        out_shape=jax.ShapeDtypeStruct(x.shape, F32),
        compiler_params=pltpu.CompilerParams(
            dimension_semantics=("arbitrary", "arbitrary"),
            vmem_limit_bytes=VMEM_LIMIT),
        name="ffn_dense",
    )(x, mod6, ng, w1, w3, w2)


def _top2(logits, n_exp):
    lane = lax.broadcasted_iota(jnp.int32, logits.shape, 1).astype(F32)
    neg = jnp.float32(-jnp.inf)
    lg = jnp.where(lane < n_exp, logits, neg)
    m1 = jnp.max(lg, axis=-1, keepdims=True)
    i1 = jnp.min(jnp.where(lg == m1, lane, float(LANES)), axis=-1, keepdims=True)
    lg2 = jnp.where(lane == i1, neg, lg)
    m2 = jnp.max(lg2, axis=-1, keepdims=True)
    i2 = jnp.min(jnp.where(lg2 == m2, lane, float(LANES)), axis=-1, keepdims=True)
    ex = jnp.exp(m2 - m1)
    return i1, i2, 1.0 / (1.0 + ex), ex / (1.0 + ex)


def _router_kernel(x_ref, mod_ref, ng_ref, rwhi_ref, rwlo_ref, h_ref, info_ref, cnt_ref, carry_ref,
                   *, n_exp):
    first = (pl.program_id(0) == 0) & (pl.program_id(1) == 0)

    @pl.when(first)
    def _():
        carry_ref[...] = jnp.zeros_like(carry_ref)

    x = x_ref[...]
    mod = mod_ref[...]
    h = _rms_mod(x, ng_ref[...], mod[3:4], mod[4:5])
    h_ref[...] = h
    h_hi = h.astype(BF16)
    h_lo = (h - h_hi.astype(F32)).astype(BF16)
    logits = (jnp.dot(h_hi, rwhi_ref[...], preferred_element_type=F32)
              + (jnp.dot(h_hi, rwlo_ref[...], preferred_element_type=F32)
                 + jnp.dot(h_lo, rwhi_ref[...], preferred_element_type=F32)))
    i1, i2, p1, p2 = _top2(logits, n_exp)
    tr = x.shape[0]
    lane = lax.broadcasted_iota(jnp.int32, (tr, LANES), 1).astype(F32)
    oh1 = lane == i1
    oh2 = lane == i2
    sel = jnp.where(oh1, 1.0, 0.0) + jnp.where(oh2, 1.0, 0.0)
    r_t = lax.broadcasted_iota(jnp.int32, (tr, tr), 0)
    c_t = lax.broadcasted_iota(jnp.int32, (tr, tr), 1)
    tri = jnp.where(r_t > c_t, 1.0, 0.0).astype(BF16)
    excl = jnp.dot(tri, sel.astype(BF16), preferred_element_type=F32)
    base = carry_ref[0:1] + excl
    rank1 = jnp.sum(jnp.where(oh1, base, 0.0), axis=-1, keepdims=True)
    rank2 = jnp.sum(jnp.where(oh2, base, 0.0), axis=-1, keepdims=True)
    info = jnp.where(lane == 0.0, i1, 0.0)
    for col, val in ((1.0, i2), (2.0, rank1), (3.0, rank2), (4.0, p1), (5.0, p2)):
        info = jnp.where(lane == col, val, info)
    info_ref[...] = info
    carry = carry_ref[...] + jnp.sum(sel, axis=0, keepdims=True)
    carry_ref[...] = carry
    cnt_ref[...] = carry


def _router(x, mod6, ng, layer, rw_p, j, n_exp):
    bsz, s, d = x.shape
    tr = ROUTER_TR
    assert s % tr == 0
    nblk = s // tr
    t = bsz * s
    kern = functools.partial(_router_kernel, n_exp=n_exp)
    return pl.pallas_call(
        kern,
        grid=(bsz, nblk),
        in_specs=[
            pl.BlockSpec((None, tr, d), lambda b, i: (b, i, 0)),
            pl.BlockSpec((None, 6, d), lambda b, i: (b, 0, 0)),
            _layer_spec(ng, layer), _layer_spec(rw_p[0], j), _layer_spec(rw_p[1], j),
        ],
        out_specs=[
            pl.BlockSpec((tr, d), lambda b, i: (b * nblk + i, 0)),
            pl.BlockSpec((tr, LANES), lambda b, i: (b * nblk + i, 0)),
            pl.BlockSpec((SUBLANES, LANES), lambda b, i: (0, 0)),
        ],
        out_shape=[
            jax.ShapeDtypeStruct((t, d), F32),
            jax.ShapeDtypeStruct((t, LANES), F32),
            jax.ShapeDtypeStruct((SUBLANES, LANES), F32),
        ],
        scratch_shapes=[pltpu.VMEM((SUBLANES, LANES), F32)],
        compiler_params=pltpu.CompilerParams(
            dimension_semantics=("arbitrary", "arbitrary"),
            vmem_limit_bytes=VMEM_LIMIT),
        name="moe_router",
    )(x, mod6, ng, *rw_p)


def _row_gather_start(src_hbm, row, dst, dst_row, sem):
    pltpu.make_async_copy(src_hbm.at[pl.ds(row, 1)], dst.at[pl.ds(dst_row, 1)], sem).start()


def _row_gather_wait(src_hbm, dst, sem):
    pltpu.make_async_copy(src_hbm.at[pl.ds(0, dst.shape[0])], dst, sem).wait()


def _expert_kernel(te_ref, nt_ref, cnt_ref, start_ref, d1_ref, d2_ref, h_hbm, w1_ref, w3_ref,
                   w2_ref, o_ref, xbuf, src_ref, sem):
    i = pl.program_id(0)
    n_tiles = nt_ref[0]
    tm = xbuf.shape[1]
    n_tok = d1_ref.shape[0]

    @pl.when(i == 0)
    def _():
        for e in range(cnt_ref.shape[0]):
            lo = start_ref[e] + cnt_ref[e]
            hi = start_ref[e] + _idiv(cnt_ref[e] + (tm - 1), tm) * tm

            def pad_body(p, carry):
                src_ref[p] = 0
                return carry

            lax.fori_loop(lo, hi, pad_body, 0)

        def inv_body(t, carry):
            src_ref[d1_ref[t]] = t
            src_ref[d2_ref[t]] = t
            return carry

        lax.fori_loop(0, n_tok, inv_body, 0, unroll=8)

        def first_body(r, carry):
            _row_gather_start(h_hbm, src_ref[r], xbuf.at[0], r, sem.at[0])
            return carry

        lax.fori_loop(0, tm, first_body, 0, unroll=8)

    @pl.when(i + 1 < n_tiles)
    def _():
        nxt_slot = (i + 1) % 2
        nxt_base = (i + 1) * tm
        for r in range(tm):
            _row_gather_start(h_hbm, src_ref[nxt_base + r], xbuf.at[nxt_slot], r, sem.at[nxt_slot])

    @pl.when(i < n_tiles)
    def _():
        slot = i % 2
        _row_gather_wait(h_hbm, xbuf.at[slot], sem.at[slot])
        xs = xbuf[slot].astype(BF16)
        o_ref[...] = _swiglu_acc(xs, w1_ref, w3_ref, w2_ref, jnp.zeros(o_ref.shape, F32))

    @pl.when(i >= n_tiles)
    def _():
        o_ref[...] = jnp.zeros_like(o_ref)


def _experts(tile_expert, n_tiles, counts, row_start, dest1, dest2, h, w1, w3, w2, j, nt_max):
    t, d = h.shape
    f = w1.shape[3]
    tm = MOE_TM
    assert f % FFN_FC == 0

    def w_map(i, te, *_):
        return (j, te[i], 0, 0)

    grid_spec = pltpu.PrefetchScalarGridSpec(
        num_scalar_prefetch=6,
        grid=(nt_max,),
        in_specs=[
            pl.BlockSpec(memory_space=pl.ANY),
            pl.BlockSpec((None, None, d, f), w_map),
            pl.BlockSpec((None, None, d, f), w_map),
            pl.BlockSpec((None, None, f, d), w_map),
        ],
        out_specs=pl.BlockSpec((tm, d), lambda i, *_: (i, 0)),
        scratch_shapes=[
            pltpu.VMEM((2, tm, d), F32),
            pltpu.SMEM((nt_max * tm,), jnp.int32),
            pltpu.SemaphoreType.DMA((2,)),
        ],
    )
    return pl.pallas_call(
        _expert_kernel,
        grid_spec=grid_spec,
        out_shape=jax.ShapeDtypeStruct((nt_max * tm, d), F32),
        compiler_params=pltpu.CompilerParams(
            dimension_semantics=("arbitrary",),
            vmem_limit_bytes=VMEM_LIMIT),
        name="moe_experts",
    )(tile_expert, n_tiles, counts, row_start, dest1, dest2, h, w1, w3, w2)


def _combine_kernel(d1_ref, d2_ref, x_ref, mod_ref, info_ref, fg_ref, y_hbm, o_ref, ybuf, sem,
                    *, final_norm):
    i = pl.program_id(0)
    n = pl.num_programs(0)
    tc = x_ref.shape[0]

    def issue(tile, slot):
        base = tile * tc
        for r in range(tc):
            _row_gather_start(y_hbm, d1_ref[base + r], ybuf.at[slot, 0], r, sem.at[slot])
            _row_gather_start(y_hbm, d2_ref[base + r], ybuf.at[slot, 1], r, sem.at[slot])

    @pl.when(i == 0)
    def _():
        issue(0, 0)

    @pl.when(i + 1 < n)
    def _():
        issue(i + 1, (i + 1) % 2)

    slot = i % 2
    _row_gather_wait(y_hbm, ybuf.at[slot, 0], sem.at[slot])
    _row_gather_wait(y_hbm, ybuf.at[slot, 1], sem.at[slot])
    info = info_ref[...]
    y = info[:, 4:5] * ybuf[slot, 0] + info[:, 5:6] * ybuf[slot, 1]
    out = x_ref[...] + mod_ref[5:6, :] * y
    if final_norm:
        ms = jnp.mean(out * out, axis=-1, keepdims=True)
        out = out * lax.rsqrt(ms + EPS) * fg_ref[...]
    o_ref[...] = out


def _combine(dest1, dest2, x, mod6, info, final_g, y, final_norm):
    bsz, s, d = x.shape
    tc = ROUTER_TR
    nblk = s // tc
    kern = functools.partial(_combine_kernel, final_norm=final_norm)
    grid_spec = pltpu.PrefetchScalarGridSpec(
        num_scalar_prefetch=2,
        grid=(bsz * nblk,),
        in_specs=[
            pl.BlockSpec((None, tc, d), lambda i, d1, d2: (i // nblk, i % nblk, 0)),
            pl.BlockSpec((None, 6, d), lambda i, d1, d2: (i // nblk, 0, 0)),
            pl.BlockSpec((tc, LANES), lambda i, d1, d2: (i, 0)),
            pl.BlockSpec(final_g.shape, lambda i, d1, d2: (0, 0)),
            pl.BlockSpec(memory_space=pl.ANY),
        ],
        out_specs=pl.BlockSpec((None, tc, d), lambda i, d1, d2: (i // nblk, i % nblk, 0)),
        scratch_shapes=[
            pltpu.VMEM((2, 2, tc, d), F32),
            pltpu.SemaphoreType.DMA((2,)),
        ],
    )
    return pl.pallas_call(
        kern,
        grid_spec=grid_spec,
        out_shape=jax.ShapeDtypeStruct(x.shape, F32),
        compiler_params=pltpu.CompilerParams(
            dimension_semantics=("arbitrary",),
            vmem_limit_bytes=VMEM_LIMIT),
        name="moe_combine",
    )(dest1, dest2, x, mod6, info, final_g, y)


def _moe(x, mod6, ng, layer, rw_p, w1, w3, w2, j, final_g, final_norm):
    bsz, s, d = x.shape
    n_exp = w1.shape[1]
    t = bsz * s
    tm = MOE_TM
    nt_max = (TOP_K * t) // tm + n_exp
    h, info, cnt = _router(x, mod6, ng, layer, rw_p, j, n_exp)

    counts = cnt[0, :n_exp].astype(jnp.int32)
    tiles_per = (counts + tm - 1) // tm
    tile_end = jnp.cumsum(tiles_per)
    row_start = (tile_end - tiles_per) * tm
    n_tiles = tile_end[-1]
    e1 = info[:, 0].astype(jnp.int32)
    e2 = info[:, 1].astype(jnp.int32)
    dest1 = row_start[e1] + info[:, 2].astype(jnp.int32)
    dest2 = row_start[e2] + info[:, 3].astype(jnp.int32)
    tile_ids = jnp.arange(nt_max, dtype=jnp.int32)
    live_ids = jnp.minimum(tile_ids, n_tiles - 1)
    tile_expert = jnp.sum((live_ids[:, None] >= tile_end[None, :]).astype(jnp.int32), axis=1)
    tile_expert = jnp.minimum(tile_expert, n_exp - 1)

    y = _experts(tile_expert, n_tiles[None].astype(jnp.int32), counts, row_start, dest1, dest2,
                 h, w1, w3, w2, j, nt_max)
    return _combine(dest1, dest2, x, mod6, info, final_g, y, final_norm)


def _final_norm_kernel(x_ref, g_ref, o_ref):
    x = x_ref[...]
    ms = jnp.mean(x * x, axis=-1, keepdims=True)
    o_ref[...] = x * lax.rsqrt(ms + EPS) * g_ref[...]


def _final_norm(x, g):
    bsz, s, d = x.shape
    tm = FFN_TM
    assert s % tm == 0
    return pl.pallas_call(
        _final_norm_kernel,
        grid=(bsz, s // tm),
        in_specs=[pl.BlockSpec((None, tm, d), lambda b, i: (b, i, 0)),
                  _const_spec(g.shape)],
        out_specs=pl.BlockSpec((None, tm, d), lambda b, i: (b, i, 0)),
        out_shape=jax.ShapeDtypeStruct(x.shape, F32),
        compiler_params=pltpu.CompilerParams(
            dimension_semantics=("arbitrary", "arbitrary"),
            vmem_limit_bytes=VMEM_LIMIT),
        name="final_norm",
    )(x, g)


def _block_diag(w):
    depth, n, a, b = w.shape
    eye = jnp.eye(n, dtype=w.dtype)
    return jnp.einsum('lnde,nm->lndme', w, eye).reshape(depth, n * a, n * b)


def kernel(x, c, ada_w, ada_b, norm_mix_g, norm_ffn_g, w_in, rg_conv_w, rg_conv_b, rg_wa, rg_ba, rg_wx, rg_bx, rg_lambda, gla_wg2, gla_bg, gla_norm_g, w_out, ffn_w1, ffn_w3, ffn_w2, router_w, moe_w1, moe_w3, moe_w2, final_g):
    bsz, s, d = x.shape
    depth = ada_w.shape[0]
    d_rg = rg_conv_w.shape[2]
    rank = gla_wg2.shape[1]
    n_exp = router_w.shape[2]
    d_main = w_in.shape[2] - rank

    mod = _adaln_mod(c, ada_w, ada_b).reshape(depth, bsz, 6, d)

    win_p = jnp.concatenate(
        [w_in[:, :, :d_main],
         jnp.pad(w_in[:, :, d_main:], ((0, 0), (0, 0), (0, LANES - rank)))], axis=2).astype(BF16)
    wg2_p = jnp.pad(gla_wg2, ((0, 0), (0, LANES - rank), (0, 0))).astype(BF16)
    wgate = jnp.concatenate([_block_diag(rg_wa), _block_diag(rg_wx)], axis=2).astype(BF16)
    bgate = jnp.concatenate([rg_ba, rg_bx], axis=1)[:, None, :]
    mixer_consts = [norm_mix_g[:, None, :], win_p, rg_conv_w, rg_conv_b[:, None, :], wgate, bgate,
                    rg_lambda[:, None, :], wg2_p, gla_bg[:, None, :], gla_norm_g[:, None, :],
                    w_out.astype(BF16)]
    ng_ffn = norm_ffn_g[:, None, :]
    ffn_w = [w.astype(BF16) for w in (ffn_w1, ffn_w3, ffn_w2)]
    moe_w = [w.astype(BF16) for w in (moe_w1, moe_w3, moe_w2)]
    rw_f32 = jnp.pad(router_w, ((0, 0), (0, 0), (0, LANES - n_exp)))
    rw_hi = rw_f32.astype(BF16)
    rw_p = (rw_hi, (rw_f32 - rw_hi.astype(F32)).astype(BF16))

    for layer in range(depth):
        x = _mixer(x, mod[layer], layer, mixer_consts)
        j = layer // 2
        if layer % 2 == 0:
            x = _ffn(x, mod[layer], ng_ffn, layer, *ffn_w, j)
        else:
            last = layer == depth - 1
            x = _moe(x, mod[layer], ng_ffn, layer, rw_p, *moe_w, j, final_g[None, :],
                     final_norm=last)
            if last:
                return x
    return _final_norm(x, final_g[None, :])
```

```python
import functools

import jax
import jax.numpy as jnp
from jax import lax
from jax.experimental import pallas as pl
from jax.experimental.pallas import tpu as pltpu

F32 = jnp.float32
BF16 = jnp.bfloat16

EPS = 1e-6
RG_C = 8.0
CONV_W = 4
RG_BLOCKS = 8
GLA_HEADS = 4
GLA_TAU = 16.0
GLA_CHUNK = 64
TOP_K = 2

LANES = 128
SUBLANES = 8
VMEM_LIMIT = 56 * 1024 * 1024
FFN_VMEM_LIMIT = 60 * 1024 * 1024

MIX_TS = 256
MIX_NSEQ = 2
FFN_TM = 512
FFN_FC = 256
ROUTER_TR = 256
MOE_TM = 512


def _sigmoid(x):
    return 1.0 / (1.0 + jnp.exp(-x))


def _rms_mod(x, g, shift, scale):
    ms = jnp.mean(x * x, axis=-1, keepdims=True)
    return (x * lax.rsqrt(ms + EPS)) * (g * (1.0 + scale)) + shift


def _log2(n):
    assert n > 0 and n & (n - 1) == 0, n
    return n.bit_length() - 1


def _idiv(x, n):
    return lax.shift_right_logical(x, _log2(n))


def _imod(x, n):
    _log2(n)
    return x & (n - 1)


def _shift_rows(x, d, fill, seg=None):
    rolled = pltpu.roll(x, d, axis=0)
    row = lax.broadcasted_iota(jnp.int32, x.shape, 0)
    if seg is not None:
        row = _imod(row, seg)
    return jnp.where(row >= d, rolled, fill)


def _linear_scan(a, b, h0):
    rows, n = a.shape
    groups = rows // SUBLANES
    a3 = a.reshape(groups, SUBLANES, n)
    b3 = b.reshape(groups, SUBLANES, n)
    sub = lax.broadcasted_iota(jnp.int32, a3.shape, 1)
    d = 1
    while d < SUBLANES:
        keep = sub >= d
        b3 = a3 * jnp.where(keep, pltpu.roll(b3, d, axis=1), 0.0) + b3
        a3 = a3 * jnp.where(keep, pltpu.roll(a3, d, axis=1), 1.0)
        d *= 2
    carry = h0
    out = []
    for gi in range(groups):
        hg = a3[gi] * carry + b3[gi]
        out.append(hg)
        carry = hg[SUBLANES - 1:SUBLANES]
    return jnp.concatenate(out, axis=0)


def _mod_kernel(c_ref, w_ref, b_ref, o_ref):
    c = c_ref[...]
    ca = c * _sigmoid(c)
    o_ref[...] = jnp.dot(ca, w_ref[...], precision=lax.Precision.HIGHEST,
                         preferred_element_type=F32) + b_ref[...]


def _adaln_mod(c, ada_w, ada_b):
    depth, d, n = ada_w.shape
    bsz = c.shape[0]
    rows = -(-bsz // SUBLANES) * SUBLANES
    c_pad = jnp.pad(c, ((0, rows - bsz), (0, 0)))
    tn = 1536
    assert n % tn == 0
    out = pl.pallas_call(
        _mod_kernel,
        grid=(depth, n // tn),
        in_specs=[
            pl.BlockSpec((rows, d), lambda l, j: (0, 0)),
            pl.BlockSpec((None, d, tn), lambda l, j: (l, 0, j)),
            pl.BlockSpec((None, 1, tn), lambda l, j: (l, 0, j)),
        ],
        out_specs=pl.BlockSpec((None, rows, tn), lambda l, j: (l, 0, j)),
        out_shape=jax.ShapeDtypeStruct((depth, rows, n), F32),
        compiler_params=pltpu.CompilerParams(
            dimension_semantics=("arbitrary", "arbitrary"),
            vmem_limit_bytes=VMEM_LIMIT),
        name="adaln_mod",
    )(c_pad, ada_w, ada_b.reshape(depth, 1, n))
    return out[:, :bsz, :]


def _mixer_kernel(x_ref, mod_ref, *rest, **dims):
    consts, (o_ref, tail_ref, hst_ref, gst_ref) = rest[:-4], rest[-4:]
    s_idx = pl.program_id(1)

    @pl.when(s_idx == 0)
    def _():
        tail_ref[...] = jnp.zeros_like(tail_ref)
        hst_ref[...] = jnp.zeros_like(hst_ref)
        gst_ref[...] = jnp.zeros_like(gst_ref)

    for k in range(x_ref.shape[0]):
        _mixer_seq(x_ref.at[k], mod_ref.at[k], *consts, o_ref.at[k], tail_ref.at[k],
                   hst_ref.at[k], gst_ref.at[k], s_idx, **dims)


def _mixer_seq(x_ref, mod_ref, ng_ref, win_ref, cw_ref, cb_ref, wg_ref, bgate_ref,
               lam_ref, wg2_ref, bg_ref, gg_ref, wout_ref, o_ref,
               tail_ref, hst_ref, gst_ref, s_idx, *, d_rg, kd, vd):
    ts = x_ref.shape[0]
    dk = kd // GLA_HEADS
    dv = vd // GLA_HEADS

    x = x_ref[...]
    mod = mod_ref[...]
    h = _rms_mod(x, ng_ref[...], mod[0:1], mod[1:2]).astype(BF16)
    z = jnp.dot(h, win_ref[...], preferred_element_type=F32)

    o0 = 0
    rg_x = z[:, o0:o0 + d_rg]; o0 += d_rg
    rg_gate = z[:, o0:o0 + d_rg]; o0 += d_rg
    q = z[:, o0:o0 + kd]; o0 += kd
    k = z[:, o0:o0 + kd]; o0 += kd
    v = z[:, o0:o0 + vd]; o0 += vd
    g = z[:, o0:o0 + vd]; o0 += vd
    f_low = z[:, o0:o0 + LANES]

    cat = jnp.concatenate([tail_ref[...], rg_x], axis=0)
    cw = cw_ref[...]
    u = cb_ref[...] + cw[CONV_W - 1:CONV_W] * rg_x
    for tap in range(CONV_W - 1):
        sh = CONV_W - 1 - tap
        u = u + cw[tap:tap + 1] * pltpu.roll(cat, sh, axis=0)[SUBLANES:SUBLANES + ts]
    tail_ref[...] = rg_x[ts - SUBLANES:ts]

    gates = jnp.dot(u.astype(BF16), wg_ref[...], preferred_element_type=F32) + bgate_ref[...]
    r = _sigmoid(gates[:, :d_rg])
    i_gate = _sigmoid(gates[:, d_rg:])
    lam = lam_ref[...]
    sp = jnp.maximum(-lam, 0.0) + jnp.log1p(jnp.exp(-jnp.abs(lam)))
    log_a = -RG_C * r * sp
    a = jnp.exp(log_a)
    mult = jnp.sqrt(-jnp.tanh(log_a) * (a * a + 1.0))
    row = lax.broadcasted_iota(jnp.int32, (ts, d_rg), 0)
    mult = jnp.where((row == 0) & (s_idx == 0), 1.0, mult)
    b_term = mult * i_gate * u
    h_rg = _linear_scan(a, b_term, hst_ref[0:1])
    hst_ref[...] = jnp.broadcast_to(h_rg[ts - 1:ts], hst_ref.shape)
    gelu = 0.5 * rg_gate * (1.0 + jnp.tanh(0.7978845608028654 * (rg_gate + 0.044715 * rg_gate ** 3)))
    rg_out = h_rg * gelu

    fz = jnp.dot(f_low.astype(BF16), wg2_ref[...], preferred_element_type=F32) + bg_ref[...]
    log_f = (jnp.minimum(fz, 0.0) - jnp.log1p(jnp.exp(-jnp.abs(fz)))) * (1.0 / GLA_TAU)
    bcum = log_f
    d = 1
    while d < GLA_CHUNK:
        bcum = bcum + _shift_rows(bcum, d, 0.0, seg=GLA_CHUNK)
        d *= 2

    scale = dk ** -0.5
    nh = GLA_HEADS
    c_len = GLA_CHUNK
    r_i = lax.broadcasted_iota(jnp.int32, (nh * c_len, kd), 0)
    c_i = lax.broadcasted_iota(jnp.int32, (nh * c_len, kd), 1)
    head_mask = _idiv(r_i, c_len) == _idiv(c_i, dk)
    r_j = lax.broadcasted_iota(jnp.int32, (nh * c_len, c_len), 0)
    c_j = lax.broadcasted_iota(jnp.int32, (nh * c_len, c_len), 1)
    causal = c_j <= _imod(r_j, c_len)
    r_s = lax.broadcasted_iota(jnp.int32, (vd, kd), 0)
    c_s = lax.broadcasted_iota(jnp.int32, (vd, kd), 1)
    bd_mask = _idiv(r_s, dv) == _idiv(c_s, dk)

    state = gst_ref[...]
    o_chunks = []
    for ci in range(ts // c_len):
        lo = ci * c_len
        bc = bcum[lo:lo + c_len]
        qc = q[lo:lo + c_len] * scale
        kc = k[lo:lo + c_len]
        vc = v[lo:lo + c_len].astype(BF16)
        b_ref_row = bc[c_len // 2 - 1:c_len // 2]
        b_last = bc[c_len - 1:c_len]
        q_loc = (qc * jnp.exp(bc - b_ref_row)).astype(BF16)
        k_loc = (kc * jnp.exp(b_ref_row - bc)).astype(BF16)
        q_stack = jnp.where(head_mask, jnp.concatenate([q_loc] * nh, axis=0), jnp.zeros((), BF16))
        scores = lax.dot_general(q_stack, k_loc, (((1,), (1,)), ((), ())),
                                 preferred_element_type=F32)
        p = jnp.where(causal, scores, 0.0).astype(BF16)
        oi = jnp.dot(p, vc, preferred_element_type=F32)
        o_intra = jnp.concatenate(
            [oi[hh * c_len:(hh + 1) * c_len, hh * dv:(hh + 1) * dv] for hh in range(nh)], axis=1)
        q_b = (qc * jnp.exp(bc)).astype(BF16)
        o_inter = lax.dot_general(q_b, state.astype(BF16), (((1,), (1,)), ((), ())),
                                  preferred_element_type=F32)
        o_chunks.append(o_intra + o_inter)
        k_end = (kc * jnp.exp(b_last - bc)).astype(BF16)
        upd = lax.dot_general(vc, k_end, (((0,), (0,)), ((), ())),
                              preferred_element_type=F32)
        state = jnp.exp(b_last) * state + jnp.where(bd_mask, upd, 0.0)
    gst_ref[...] = state
    o = jnp.concatenate(o_chunks, axis=0)

    gg = gg_ref[...]
    heads = []
    for hh in range(nh):
        oh = o[:, hh * dv:(hh + 1) * dv]
        ms = jnp.mean(oh * oh, axis=-1, keepdims=True)
        heads.append(oh * lax.rsqrt(ms + EPS) * gg)
    gla_out = jnp.concatenate(heads, axis=1) * (g * _sigmoid(g))

    mix_in = jnp.concatenate([rg_out, gla_out], axis=1).astype(BF16)
    mix = jnp.dot(mix_in, wout_ref[...], preferred_element_type=F32)
    o_ref[...] = x + mod[2:3] * mix


def _const_spec(shape):
    nd = len(shape)
    return pl.BlockSpec(shape, lambda *_: (0,) * nd, pipeline_mode=pl.Buffered(1))


def _layer_spec(stacked, layer):
    rest = stacked.shape[1:]
    return pl.BlockSpec((None,) + rest, lambda *_: (layer,) + (0,) * len(rest),
                        pipeline_mode=pl.Buffered(1))


def _mixer(x, mod6, layer, consts):
    bsz, s, d = x.shape
    d_rg = consts[2].shape[2]
    kd = consts[8].shape[2]
    vd = consts[10].shape[1] - d_rg
    ts = MIX_TS
    nseq = MIX_NSEQ if bsz % MIX_NSEQ == 0 else 1
    assert s % ts == 0 and ts % GLA_CHUNK == 0
    kern = functools.partial(_mixer_kernel, d_rg=d_rg, kd=kd, vd=vd)
    return pl.pallas_call(
        kern,
        grid=(bsz // nseq, s // ts),
        in_specs=[
            pl.BlockSpec((nseq, ts, d), lambda b, i: (b, i, 0)),
            pl.BlockSpec((nseq, 6, d), lambda b, i: (b, 0, 0)),
        ] + [_layer_spec(a, layer) for a in consts],
        out_specs=pl.BlockSpec((nseq, ts, d), lambda b, i: (b, i, 0)),
        out_shape=jax.ShapeDtypeStruct(x.shape, F32),
        scratch_shapes=[
            pltpu.VMEM((nseq, SUBLANES, d_rg), F32),
            pltpu.VMEM((nseq, SUBLANES, d_rg), F32),
            pltpu.VMEM((nseq, vd, kd), F32),
        ],
        compiler_params=pltpu.CompilerParams(
            dimension_semantics=("arbitrary", "arbitrary"),
            vmem_limit_bytes=VMEM_LIMIT),
        name="mixer",
    )(x, mod6, *consts)


def _swiglu_acc(h, w1_ref, w3_ref, w2_ref, acc):
    f = w1_ref.shape[-1]
    for j in range(f // FFN_FC):
        sl = slice(j * FFN_FC, (j + 1) * FFN_FC)
        a = jnp.dot(h, w1_ref[:, sl], preferred_element_type=F32)
        b = jnp.dot(h, w3_ref[:, sl], preferred_element_type=F32)
        p = (a * _sigmoid(a) * b).astype(BF16)
        acc = acc + jnp.dot(p, w2_ref[sl, :], preferred_element_type=F32)
    return acc


def _ffn_kernel(x_ref, mod_ref, ng_ref, w1_ref, w3_ref, w2_ref, *rest):
    n_cast = (len(rest) - 1) // 2
    cast_in, o_ref, cast_out = rest[:n_cast], rest[n_cast], rest[n_cast + 1:]
    x = x_ref[...]
    mod = mod_ref[...]
    h = _rms_mod(x, ng_ref[...], mod[3:4], mod[4:5]).astype(BF16)
    acc = _swiglu_acc(h, w1_ref, w3_ref, w2_ref, jnp.zeros(x.shape, F32))
    o_ref[...] = x + mod[5:6] * acc
    for src, dst in zip(cast_in, cast_out):
        dst[...] = src[...].astype(BF16)


def _ffn(x, mod6, ng, layer, w1, w3, w2, j, cast_src, cast_layer):
    bsz, s, d = x.shape
    f = w1.shape[2]
    tm = FFN_TM
    assert s % tm == 0 and f % FFN_FC == 0
    nblk = s // tm
    steps = bsz * nblk
    slabs = []
    for w in cast_src:
        n_layers, n_exp, rows, cols = w.shape
        assert (n_exp * rows) % steps == 0
        slabs.append(w.reshape(n_layers, steps, n_exp * rows // steps, cols))
    outs = pl.pallas_call(
        _ffn_kernel,
        grid=(bsz, nblk),
        in_specs=[
            pl.BlockSpec((None, tm, d), lambda b, i: (b, i, 0)),
            pl.BlockSpec((None, 6, d), lambda b, i: (b, 0, 0)),
            _layer_spec(ng, layer), _layer_spec(w1, j), _layer_spec(w3, j), _layer_spec(w2, j),
        ] + [pl.BlockSpec((None, None) + w.shape[2:], lambda b, i: (cast_layer, b * nblk + i, 0, 0))
             for w in slabs],
        out_specs=[pl.BlockSpec((None, tm, d), lambda b, i: (b, i, 0))]
        + [pl.BlockSpec((None,) + w.shape[2:], lambda b, i: (b * nblk + i, 0, 0)) for w in slabs],
        out_shape=[jax.ShapeDtypeStruct(x.shape, F32)]
        + [jax.ShapeDtypeStruct(w.shape[1:], BF16) for w in slabs],
        compiler_params=pltpu.CompilerParams(
            dimension_semantics=("arbitrary", "arbitrary"),
            vmem_limit_bytes=FFN_VMEM_LIMIT),
        name="ffn_dense",
    )(x, mod6, ng, w1, w3, w2, *slabs)
    cast = [o.reshape(w.shape[1:]) for o, w in zip(outs[1:], cast_src)]
    return outs[0], cast


def _top2(logits, n_exp):
    lane = lax.broadcasted_iota(jnp.int32, logits.shape, 1).astype(F32)
    neg = jnp.float32(-jnp.inf)
    lg = jnp.where(lane < n_exp, logits, neg)
    m1 = jnp.max(lg, axis=-1, keepdims=True)
    i1 = jnp.min(jnp.where(lg == m1, lane, float(LANES)), axis=-1, keepdims=True)
    lg2 = jnp.where(lane == i1, neg, lg)
    m2 = jnp.max(lg2, axis=-1, keepdims=True)
    i2 = jnp.min(jnp.where(lg2 == m2, lane, float(LANES)), axis=-1, keepdims=True)
    ex = jnp.exp(m2 - m1)
    return i1, i2, 1.0 / (1.0 + ex), ex / (1.0 + ex)


def _router_kernel(x_ref, mod_ref, ng_ref, rwhi_ref, rwlo_ref, h_ref, info_ref, meta_ref, cnt_ref,
                   carry_ref, *, n_exp):
    first = (pl.program_id(0) == 0) & (pl.program_id(1) == 0)

    @pl.when(first)
    def _():
        carry_ref[...] = jnp.zeros_like(carry_ref)

    x = x_ref[...]
    mod = mod_ref[...]
    h = _rms_mod(x, ng_ref[...], mod[3:4], mod[4:5])
    h_ref[...] = h
    h_hi = h.astype(BF16)
    h_lo = (h - h_hi.astype(F32)).astype(BF16)
    logits = (jnp.dot(h_hi, rwhi_ref[...], preferred_element_type=F32)
              + (jnp.dot(h_hi, rwlo_ref[...], preferred_element_type=F32)
                 + jnp.dot(h_lo, rwhi_ref[...], preferred_element_type=F32)))
    i1, i2, p1, p2 = _top2(logits, n_exp)
    tr = x.shape[0]
    lane = lax.broadcasted_iota(jnp.int32, (tr, LANES), 1).astype(F32)
    oh1 = lane == i1
    oh2 = lane == i2
    sel = jnp.where(oh1, 1.0, 0.0) + jnp.where(oh2, 1.0, 0.0)
    r_t = lax.broadcasted_iota(jnp.int32, (tr, tr), 0)
    c_t = lax.broadcasted_iota(jnp.int32, (tr, tr), 1)
    tri = jnp.where(r_t > c_t, 1.0, 0.0).astype(BF16)
    excl = jnp.dot(tri, sel.astype(BF16), preferred_element_type=F32)
    base = carry_ref[0:1] + excl
    rank1 = jnp.sum(jnp.where(oh1, base, 0.0), axis=-1, keepdims=True)
    rank2 = jnp.sum(jnp.where(oh2, base, 0.0), axis=-1, keepdims=True)
    info = jnp.where(lane == 0.0, i1, 0.0)
    for col, val in ((1.0, i2), (2.0, rank1), (3.0, rank2), (4.0, p1), (5.0, p2)):
        info = jnp.where(lane == col, val, info)
    info_ref[...] = info
    meta_ref[...] = jnp.transpose(info)[:SUBLANES]
    carry = carry_ref[...] + jnp.sum(sel, axis=0, keepdims=True)
    carry_ref[...] = carry
    cnt_ref[...] = carry


def _router(x, mod6, ng, layer, rw_p, j, n_exp):
    bsz, s, d = x.shape
    tr = ROUTER_TR
    assert s % tr == 0
    nblk = s // tr
    t = bsz * s
    kern = functools.partial(_router_kernel, n_exp=n_exp)
    return pl.pallas_call(
        kern,
        grid=(bsz, nblk),
        in_specs=[
            pl.BlockSpec((None, tr, d), lambda b, i: (b, i, 0)),
            pl.BlockSpec((None, 6, d), lambda b, i: (b, 0, 0)),
            _layer_spec(ng, layer), _layer_spec(rw_p[0], j), _layer_spec(rw_p[1], j),
        ],
        out_specs=[
            pl.BlockSpec((tr, d), lambda b, i: (b * nblk + i, 0)),
            pl.BlockSpec((tr, LANES), lambda b, i: (b * nblk + i, 0)),
            pl.BlockSpec((SUBLANES, tr), lambda b, i: (0, b * nblk + i)),
            pl.BlockSpec((SUBLANES, LANES), lambda b, i: (0, 0)),
        ],
        out_shape=[
            jax.ShapeDtypeStruct((t, d), F32),
            jax.ShapeDtypeStruct((t, LANES), F32),
            jax.ShapeDtypeStruct((SUBLANES, t), F32),
            jax.ShapeDtypeStruct((SUBLANES, LANES), F32),
        ],
        scratch_shapes=[pltpu.VMEM((SUBLANES, LANES), F32)],
        compiler_params=pltpu.CompilerParams(
            dimension_semantics=("arbitrary", "arbitrary"),
            vmem_limit_bytes=VMEM_LIMIT),
        name="moe_router",
    )(x, mod6, ng, *rw_p)


def _row_gather_start(src_hbm, row, dst, dst_row, sem):
    pltpu.make_async_copy(src_hbm.at[pl.ds(row, 1)], dst.at[pl.ds(dst_row, 1)], sem).start()


def _row_gather_wait(src_hbm, dst, sem):
    pltpu.make_async_copy(src_hbm.at[pl.ds(0, dst.shape[0])], dst, sem).wait()


def _expert_kernel(te_ref, nt_ref, cnt_ref, start_ref, d1_ref, d2_ref, h_hbm, w1_ref, w3_ref,
                   w2_ref, o_ref, xbuf, src_ref, sem):
    i = pl.program_id(0)
    n_tiles = nt_ref[0]
    tm = xbuf.shape[1]
    n_tok = d1_ref.shape[0]

    @pl.when(i == 0)
    def _():
        for e in range(cnt_ref.shape[0]):
            lo = start_ref[e] + cnt_ref[e]
            hi = start_ref[e] + _idiv(cnt_ref[e] + (tm - 1), tm) * tm

            def pad_body(p, carry):
                src_ref[p] = 0
                return carry

            lax.fori_loop(lo, hi, pad_body, 0)

        def inv_body(t, carry):
            src_ref[d1_ref[t]] = t
            src_ref[d2_ref[t]] = t
            return carry

        lax.fori_loop(0, n_tok, inv_body, 0, unroll=8)

        def first_body(r, carry):
            _row_gather_start(h_hbm, src_ref[r], xbuf.at[0], r, sem.at[0])
            return carry

        lax.fori_loop(0, tm, first_body, 0, unroll=8)

    @pl.when(i + 1 < n_tiles)
    def _():
        nxt_slot = (i + 1) % 2
        nxt_base = (i + 1) * tm
        for r in range(tm):
            _row_gather_start(h_hbm, src_ref[nxt_base + r], xbuf.at[nxt_slot], r, sem.at[nxt_slot])

    @pl.when(i < n_tiles)
    def _():
        slot = i % 2
        _row_gather_wait(h_hbm, xbuf.at[slot], sem.at[slot])
        xs = xbuf[slot].astype(BF16)
        o_ref[...] = _swiglu_acc(xs, w1_ref, w3_ref, w2_ref, jnp.zeros(o_ref.shape, F32))

    @pl.when(i >= n_tiles)
    def _():
        o_ref[...] = jnp.zeros_like(o_ref)


def _experts(tile_expert, n_tiles, counts, row_start, dest1, dest2, h, w1, w3, w2, nt_max):
    t, d = h.shape
    f = w1.shape[2]
    tm = MOE_TM
    assert f % FFN_FC == 0

    def w_map(i, te, *_):
        return (te[i], 0, 0)

    grid_spec = pltpu.PrefetchScalarGridSpec(
        num_scalar_prefetch=6,
        grid=(nt_max,),
        in_specs=[
            pl.BlockSpec(memory_space=pl.ANY),
            pl.BlockSpec((None, d, f), w_map),
            pl.BlockSpec((None, d, f), w_map),
            pl.BlockSpec((None, f, d), w_map),
        ],
        out_specs=pl.BlockSpec((tm, d), lambda i, *_: (i, 0)),
        scratch_shapes=[
            pltpu.VMEM((2, tm, d), F32),
            pltpu.SMEM((nt_max * tm,), jnp.int32),
            pltpu.SemaphoreType.DMA((2,)),
        ],
    )
    return pl.pallas_call(
        _expert_kernel,
        grid_spec=grid_spec,
        out_shape=jax.ShapeDtypeStruct((nt_max * tm, d), F32),
        compiler_params=pltpu.CompilerParams(
            dimension_semantics=("arbitrary",),
            vmem_limit_bytes=VMEM_LIMIT),
        name="moe_experts",
    )(tile_expert, n_tiles, counts, row_start, dest1, dest2, h, w1, w3, w2)


def _combine_kernel(d1_ref, d2_ref, x_ref, mod_ref, info_ref, fg_ref, y_hbm, o_ref, ybuf, sem,
                    *, final_norm):
    i = pl.program_id(0)
    n = pl.num_programs(0)
    tc = x_ref.shape[0]

    def issue(tile, slot):
        base = tile * tc
        for r in range(tc):
            _row_gather_start(y_hbm, d1_ref[base + r], ybuf.at[slot, 0], r, sem.at[slot])
            _row_gather_start(y_hbm, d2_ref[base + r], ybuf.at[slot, 1], r, sem.at[slot])

    @pl.when(i == 0)
    def _():
        issue(0, 0)

    @pl.when(i + 1 < n)
    def _():
        issue(i + 1, (i + 1) % 2)

    slot = i % 2
    _row_gather_wait(y_hbm, ybuf.at[slot, 0], sem.at[slot])
    _row_gather_wait(y_hbm, ybuf.at[slot, 1], sem.at[slot])
    info = info_ref[...]
    y = info[:, 4:5] * ybuf[slot, 0] + info[:, 5:6] * ybuf[slot, 1]
    out = x_ref[...] + mod_ref[5:6, :] * y
    if final_norm:
        ms = jnp.mean(out * out, axis=-1, keepdims=True)
        out = out * lax.rsqrt(ms + EPS) * fg_ref[...]
    o_ref[...] = out


def _combine(dest1, dest2, x, mod6, info, final_g, y, final_norm):
    bsz, s, d = x.shape
    tc = ROUTER_TR
    nblk = s // tc
    kern = functools.partial(_combine_kernel, final_norm=final_norm)
    grid_spec = pltpu.PrefetchScalarGridSpec(
        num_scalar_prefetch=2,
        grid=(bsz * nblk,),
        in_specs=[
            pl.BlockSpec((None, tc, d), lambda i, d1, d2: (i // nblk, i % nblk, 0)),
            pl.BlockSpec((None, 6, d), lambda i, d1, d2: (i // nblk, 0, 0)),
            pl.BlockSpec((tc, LANES), lambda i, d1, d2: (i, 0)),
            pl.BlockSpec(final_g.shape, lambda i, d1, d2: (0, 0)),
            pl.BlockSpec(memory_space=pl.ANY),
        ],
        out_specs=pl.BlockSpec((None, tc, d), lambda i, d1, d2: (i // nblk, i % nblk, 0)),
        scratch_shapes=[
            pltpu.VMEM((2, 2, tc, d), F32),
            pltpu.SemaphoreType.DMA((2,)),
        ],
    )
    return pl.pallas_call(
        kern,
        grid_spec=grid_spec,
        out_shape=jax.ShapeDtypeStruct(x.shape, F32),
        compiler_params=pltpu.CompilerParams(
            dimension_semantics=("arbitrary",),
            vmem_limit_bytes=VMEM_LIMIT),
        name="moe_combine",
    )(dest1, dest2, x, mod6, info, final_g, y)


def _moe(x, mod6, ng, layer, rw_p, w1, w3, w2, j, final_g, final_norm):
    bsz, s, d = x.shape
    n_exp = w1.shape[0]
    t = bsz * s
    tm = MOE_TM
    nt_max = (TOP_K * t) // tm + n_exp
    h, info, meta, cnt = _router(x, mod6, ng, layer, rw_p, j, n_exp)

    counts = cnt[0, :n_exp].astype(jnp.int32)
    tiles_per = (counts + tm - 1) // tm
    tile_end = jnp.cumsum(tiles_per)
    row_start = (tile_end - tiles_per) * tm
    n_tiles = tile_end[-1]
    meta = meta.astype(jnp.int32)
    dest1 = row_start[meta[0]] + meta[2]
    dest2 = row_start[meta[1]] + meta[3]
    tile_ids = jnp.arange(nt_max, dtype=jnp.int32)
    live_ids = jnp.minimum(tile_ids, n_tiles - 1)
    tile_expert = jnp.sum((live_ids[:, None] >= tile_end[None, :]).astype(jnp.int32), axis=1)
    tile_expert = jnp.minimum(tile_expert, n_exp - 1)

    y = _experts(tile_expert, n_tiles[None].astype(jnp.int32), counts, row_start, dest1, dest2,
                 h, w1, w3, w2, nt_max)
    return _combine(dest1, dest2, x, mod6, info, final_g, y, final_norm)


def _final_norm_kernel(x_ref, g_ref, o_ref):
    x = x_ref[...]
    ms = jnp.mean(x * x, axis=-1, keepdims=True)
    o_ref[...] = x * lax.rsqrt(ms + EPS) * g_ref[...]


def _final_norm(x, g):
    bsz, s, d = x.shape
    tm = FFN_TM
    assert s % tm == 0
    return pl.pallas_call(
        _final_norm_kernel,
        grid=(bsz, s // tm),
        in_specs=[pl.BlockSpec((None, tm, d), lambda b, i: (b, i, 0)),
                  _const_spec(g.shape)],
        out_specs=pl.BlockSpec((None, tm, d), lambda b, i: (b, i, 0)),
        out_shape=jax.ShapeDtypeStruct(x.shape, F32),
        compiler_params=pltpu.CompilerParams(
            dimension_semantics=("arbitrary", "arbitrary"),
            vmem_limit_bytes=VMEM_LIMIT),
        name="final_norm",
    )(x, g)


def _block_diag(w):
    depth, n, a, b = w.shape
    eye = jnp.eye(n, dtype=w.dtype)
    return jnp.einsum('lnde,nm->lndme', w, eye).reshape(depth, n * a, n * b)


def kernel(x, c, ada_w, ada_b, norm_mix_g, norm_ffn_g, w_in, rg_conv_w, rg_conv_b, rg_wa, rg_ba, rg_wx, rg_bx, rg_lambda, gla_wg2, gla_bg, gla_norm_g, w_out, ffn_w1, ffn_w3, ffn_w2, router_w, moe_w1, moe_w3, moe_w2, final_g):
    bsz, s, d = x.shape
    depth = ada_w.shape[0]
    d_rg = rg_conv_w.shape[2]
    rank = gla_wg2.shape[1]
    n_exp = router_w.shape[2]
    d_main = w_in.shape[2] - rank

    mod = _adaln_mod(c, ada_w, ada_b).reshape(depth, bsz, 6, d)

    win_p = jnp.concatenate(
        [w_in[:, :, :d_main],
         jnp.pad(w_in[:, :, d_main:], ((0, 0), (0, 0), (0, LANES - rank)))], axis=2).astype(BF16)
    wg2_p = jnp.pad(gla_wg2, ((0, 0), (0, LANES - rank), (0, 0))).astype(BF16)
    wgate = jnp.concatenate([_block_diag(rg_wa), _block_diag(rg_wx)], axis=2).astype(BF16)
    bgate = jnp.concatenate([rg_ba, rg_bx], axis=1)[:, None, :]
    mixer_consts = [norm_mix_g[:, None, :], win_p, rg_conv_w, rg_conv_b[:, None, :], wgate, bgate,
                    rg_lambda[:, None, :], wg2_p, gla_bg[:, None, :], gla_norm_g[:, None, :],
                    w_out.astype(BF16)]
    ng_ffn = norm_ffn_g[:, None, :]
    ffn_w = [w.astype(BF16) for w in (ffn_w1, ffn_w3, ffn_w2)]
    moe_w_f32 = (moe_w1, moe_w3, moe_w2)
    moe_w = None
    rw_f32 = jnp.pad(router_w, ((0, 0), (0, 0), (0, LANES - n_exp)))
    rw_hi = rw_f32.astype(BF16)
    rw_p = (rw_hi, (rw_f32 - rw_hi.astype(F32)).astype(BF16))

    for layer in range(depth):
        x = _mixer(x, mod[layer], layer, mixer_consts)
        j = layer // 2
        if layer % 2 == 0:
            cast_src = moe_w_f32 if layer + 1 < depth else ()
            x, moe_w = _ffn(x, mod[layer], ng_ffn, layer, *ffn_w, j, cast_src, j)
        else:
            last = layer == depth - 1
            x = _moe(x, mod[layer], ng_ffn, layer, rw_p, *moe_w, j, final_g[None, :],
                     final_norm=last)
            if last:
                return x
    return _final_norm(x, final_g[None, :])
```

```python
import functools

import jax
import jax.numpy as jnp
from jax import lax
from jax.experimental import pallas as pl
from jax.experimental.pallas import tpu as pltpu

F32 = jnp.float32
BF16 = jnp.bfloat16

EPS = 1e-6
RG_C = 8.0
CONV_W = 4
RG_BLOCKS = 8
GLA_HEADS = 4
GLA_TAU = 16.0
GLA_CHUNK = 64
TOP_K = 2

LANES = 128
SUBLANES = 8
VMEM_LIMIT = 56 * 1024 * 1024
FFN_VMEM_LIMIT = 60 * 1024 * 1024

MIX_TS = 256
MIX_NSEQ = 4
FFN_TM = 512
FFN_FC = 256
ROUTER_TR = 256
MOE_TM = 512


def _sigmoid(x):
    return 1.0 / (1.0 + jnp.exp(-x))


def _rms_mod(x, g, shift, scale):
    ms = jnp.mean(x * x, axis=-1, keepdims=True)
    return (x * lax.rsqrt(ms + EPS)) * (g * (1.0 + scale)) + shift


def _log2(n):
    assert n > 0 and n & (n - 1) == 0, n
    return n.bit_length() - 1


def _idiv(x, n):
    return lax.shift_right_logical(x, _log2(n))


def _imod(x, n):
    _log2(n)
    return x & (n - 1)


def _shift_rows(x, d, fill, seg=None):
    rolled = pltpu.roll(x, d, axis=0)
    row = lax.broadcasted_iota(jnp.int32, x.shape, 0)
    if seg is not None:
        row = _imod(row, seg)
    return jnp.where(row >= d, rolled, fill)


def _linear_scan(a, b, h0):
    rows, n = a.shape
    groups = rows // SUBLANES
    a3 = a.reshape(groups, SUBLANES, n)
    b3 = b.reshape(groups, SUBLANES, n)
    sub = lax.broadcasted_iota(jnp.int32, a3.shape, 1)
    d = 1
    while d < SUBLANES:
        keep = sub >= d
        b3 = a3 * jnp.where(keep, pltpu.roll(b3, d, axis=1), 0.0) + b3
        a3 = a3 * jnp.where(keep, pltpu.roll(a3, d, axis=1), 1.0)
        d *= 2
    carry = h0
    out = []
    for gi in range(groups):
        hg = a3[gi] * carry + b3[gi]
        out.append(hg)
        carry = hg[SUBLANES - 1:SUBLANES]
    return jnp.concatenate(out, axis=0)


def _mod_kernel(c_ref, w_ref, b_ref, o_ref):
    kk = pl.program_id(1)
    c = c_ref[...]
    ca = c * _sigmoid(c)
    part = jnp.dot(ca, w_ref[...], precision=lax.Precision.HIGHEST, preferred_element_type=F32)

    @pl.when(kk == 0)
    def _():
        o_ref[...] = part + b_ref[...]

    @pl.when(kk != 0)
    def _():
        o_ref[...] += part


def _adaln_mod(c, ada_w, ada_b):
    depth, d, n = ada_w.shape
    bsz = c.shape[0]
    rows = -(-bsz // SUBLANES) * SUBLANES
    c_pad = jnp.pad(c, ((0, rows - bsz), (0, 0)))
    tk = 256
    assert d % tk == 0
    out = pl.pallas_call(
        _mod_kernel,
        grid=(depth, d // tk),
        in_specs=[
            pl.BlockSpec((rows, tk), lambda l, k: (0, k)),
            pl.BlockSpec((None, tk, n), lambda l, k: (l, k, 0)),
            pl.BlockSpec((None, 1, n), lambda l, k: (l, 0, 0)),
        ],
        out_specs=pl.BlockSpec((None, rows, n), lambda l, k: (l, 0, 0)),
        out_shape=jax.ShapeDtypeStruct((depth, rows, n), F32),
        compiler_params=pltpu.CompilerParams(
            dimension_semantics=("arbitrary", "arbitrary"),
            vmem_limit_bytes=VMEM_LIMIT),
        name="adaln_mod",
    )(c_pad, ada_w, ada_b.reshape(depth, 1, n))
    return out[:, :bsz, :]


def _mixer_kernel(x_ref, mod_ref, *rest, **dims):
    consts, (o_ref, tail_ref, hst_ref, gst_ref) = rest[:-4], rest[-4:]
    s_idx = pl.program_id(1)

    @pl.when(s_idx == 0)
    def _():
        tail_ref[...] = jnp.zeros_like(tail_ref)
        hst_ref[...] = jnp.zeros_like(hst_ref)
        gst_ref[...] = jnp.zeros_like(gst_ref)

    ng_ref, win_ref = consts[0], consts[1]
    nseq = x_ref.shape[0]
    widths = (dims["d_rg"], dims["d_rg"], dims["kd"], dims["kd"], dims["vd"], dims["vd"], LANES)
    proj = [[] for _ in range(nseq)]
    for k in range(nseq + 1):
        work = []
        if k < nseq:
            work.append((_project_in(x_ref.at[k], mod_ref.at[k], ng_ref, win_ref, widths, proj[k]), 1))
        if k >= 1:
            j = k - 1
            work.append((_mixer_seq(proj[j], x_ref.at[j], mod_ref.at[j], *consts[2:], o_ref.at[j],
                                    tail_ref.at[j], hst_ref.at[j], gst_ref.at[j], s_idx, **dims), 1))
        _alternate(work)


def _alternate(work):
    work = list(work)
    while work:
        for item in list(work):
            gen, n = item
            for _ in range(n):
                try:
                    next(gen)
                except StopIteration:
                    work.remove(item)
                    break


def _project_in(x_ref, mod_ref, ng_ref, win_ref, widths, out):
    mod = mod_ref[...]
    h = _rms_mod(x_ref[...], ng_ref[...], mod[0:1], mod[1:2]).astype(BF16)
    yield
    o0 = 0
    for w in widths:
        parts = []
        for c in range(0, w, 256):
            cw = min(256, w - c)
            parts.append(jnp.dot(h, win_ref[:, o0 + c:o0 + c + cw], preferred_element_type=F32))
            yield
        out.append(parts[0] if len(parts) == 1 else jnp.concatenate(parts, axis=1))
        o0 += w


def _mixer_seq(proj, x_ref, mod_ref, cw_ref, cb_ref, wg_ref, bgate_ref,
               lam_ref, wg2_ref, bg_ref, gg_ref, wout_ref, o_ref,
               tail_ref, hst_ref, gst_ref, s_idx, *, d_rg, kd, vd):
    ts = x_ref.shape[0]
    dk = kd // GLA_HEADS
    dv = vd // GLA_HEADS
    rg_x, rg_gate, q, k, v, g, f_low = proj

    cat = jnp.concatenate([tail_ref[...], rg_x], axis=0)
    cw = cw_ref[...]
    u = cb_ref[...] + cw[CONV_W - 1:CONV_W] * rg_x
    for tap in range(CONV_W - 1):
        sh = CONV_W - 1 - tap
        u = u + cw[tap:tap + 1] * pltpu.roll(cat, sh, axis=0)[SUBLANES:SUBLANES + ts]
    tail_ref[...] = rg_x[ts - SUBLANES:ts]
    yield

    gates = jnp.dot(u.astype(BF16), wg_ref[...], preferred_element_type=F32) + bgate_ref[...]
    yield
    r = _sigmoid(gates[:, :d_rg])
    i_gate = _sigmoid(gates[:, d_rg:])
    lam = lam_ref[...]
    sp = jnp.maximum(-lam, 0.0) + jnp.log1p(jnp.exp(-jnp.abs(lam)))
    log_a = -RG_C * r * sp
    a = jnp.exp(log_a)
    mult = jnp.sqrt(-jnp.tanh(log_a) * (a * a + 1.0))
    row = lax.broadcasted_iota(jnp.int32, (ts, d_rg), 0)
    mult = jnp.where((row == 0) & (s_idx == 0), 1.0, mult)
    b_term = mult * i_gate * u
    yield
    h_rg = _linear_scan(a, b_term, hst_ref[0:1])
    hst_ref[...] = jnp.broadcast_to(h_rg[ts - 1:ts], hst_ref.shape)
    gelu = 0.5 * rg_gate * (1.0 + jnp.tanh(0.7978845608028654 * (rg_gate + 0.044715 * rg_gate ** 3)))
    rg_out = h_rg * gelu
    yield

    fz = jnp.dot(f_low.astype(BF16), wg2_ref[...], preferred_element_type=F32) + bg_ref[...]
    log_f = (jnp.minimum(fz, 0.0) - jnp.log1p(jnp.exp(-jnp.abs(fz)))) * (1.0 / GLA_TAU)
    bcum = log_f
    d = 1
    while d < GLA_CHUNK:
        bcum = bcum + _shift_rows(bcum, d, 0.0, seg=GLA_CHUNK)
        d *= 2

    yield
    scale = dk ** -0.5
    nh = GLA_HEADS
    c_len = GLA_CHUNK
    r_i = lax.broadcasted_iota(jnp.int32, (nh * c_len, kd), 0)
    c_i = lax.broadcasted_iota(jnp.int32, (nh * c_len, kd), 1)
    head_mask = _idiv(r_i, c_len) == _idiv(c_i, dk)
    r_j = lax.broadcasted_iota(jnp.int32, (nh * c_len, c_len), 0)
    c_j = lax.broadcasted_iota(jnp.int32, (nh * c_len, c_len), 1)
    causal = c_j <= _imod(r_j, c_len)
    r_s = lax.broadcasted_iota(jnp.int32, (vd, kd), 0)
    c_s = lax.broadcasted_iota(jnp.int32, (vd, kd), 1)
    bd_mask = _idiv(r_s, dv) == _idiv(c_s, dk)

    state = gst_ref[...]
    o_chunks = []
    for ci in range(ts // c_len):
        lo = ci * c_len
        bc = bcum[lo:lo + c_len]
        qc = q[lo:lo + c_len] * scale
        kc = k[lo:lo + c_len]
        vc = v[lo:lo + c_len].astype(BF16)
        b_ref_row = bc[c_len // 2 - 1:c_len // 2]
        b_last = bc[c_len - 1:c_len]
        q_loc = (qc * jnp.exp(bc - b_ref_row)).astype(BF16)
        k_loc = (kc * jnp.exp(b_ref_row - bc)).astype(BF16)
        q_stack = jnp.where(head_mask, jnp.concatenate([q_loc] * nh, axis=0), jnp.zeros((), BF16))
        scores = lax.dot_general(q_stack, k_loc, (((1,), (1,)), ((), ())),
                                 preferred_element_type=F32)
        p = jnp.where(causal, scores, 0.0).astype(BF16)
        oi = jnp.dot(p, vc, preferred_element_type=F32)
        o_intra = jnp.concatenate(
            [oi[hh * c_len:(hh + 1) * c_len, hh * dv:(hh + 1) * dv] for hh in range(nh)], axis=1)
        q_b = (qc * jnp.exp(bc)).astype(BF16)
        o_inter = lax.dot_general(q_b, state.astype(BF16), (((1,), (1,)), ((), ())),
                                  preferred_element_type=F32)
        o_chunks.append(o_intra + o_inter)
        k_end = (kc * jnp.exp(b_last - bc)).astype(BF16)
        upd = lax.dot_general(vc, k_end, (((0,), (0,)), ((), ())),
                              preferred_element_type=F32)
        state = jnp.exp(b_last) * state + jnp.where(bd_mask, upd, 0.0)
        yield
    gst_ref[...] = state
    o = jnp.concatenate(o_chunks, axis=0)

    gg = gg_ref[...]
    heads = []
    for hh in range(nh):
        oh = o[:, hh * dv:(hh + 1) * dv]
        ms = jnp.mean(oh * oh, axis=-1, keepdims=True)
        heads.append(oh * lax.rsqrt(ms + EPS) * gg)
    gla_out = jnp.concatenate(heads, axis=1) * (g * _sigmoid(g))
    yield

    mix_in = jnp.concatenate([rg_out, gla_out], axis=1).astype(BF16)
    mix = jnp.dot(mix_in, wout_ref[...], preferred_element_type=F32)
    o_ref[...] = x_ref[...] + mod_ref[2:3, :] * mix


def _const_spec(shape):
    nd = len(shape)
    return pl.BlockSpec(shape, lambda *_: (0,) * nd, pipeline_mode=pl.Buffered(1))


def _layer_spec(stacked, layer):
    rest = stacked.shape[1:]
    return pl.BlockSpec((None,) + rest, lambda *_: (layer,) + (0,) * len(rest),
                        pipeline_mode=pl.Buffered(1))


def _mixer(x, mod6, layer, consts):
    bsz, s, d = x.shape
    d_rg = consts[2].shape[2]
    kd = consts[8].shape[2]
    vd = consts[10].shape[1] - d_rg
    ts = MIX_TS
    nseq = MIX_NSEQ if bsz % MIX_NSEQ == 0 else 1
    assert s % ts == 0 and ts % GLA_CHUNK == 0
    kern = functools.partial(_mixer_kernel, d_rg=d_rg, kd=kd, vd=vd)
    return pl.pallas_call(
        kern,
        grid=(bsz // nseq, s // ts),
        in_specs=[
            pl.BlockSpec((nseq, ts, d), lambda b, i: (b, i, 0)),
            pl.BlockSpec((nseq, 6, d), lambda b, i: (b, 0, 0)),
        ] + [_layer_spec(a, layer) for a in consts],
        out_specs=pl.BlockSpec((nseq, ts, d), lambda b, i: (b, i, 0)),
        out_shape=jax.ShapeDtypeStruct(x.shape, F32),
        scratch_shapes=[
            pltpu.VMEM((nseq, SUBLANES, d_rg), F32),
            pltpu.VMEM((nseq, SUBLANES, d_rg), F32),
            pltpu.VMEM((nseq, vd, kd), F32),
        ],
        compiler_params=pltpu.CompilerParams(
            dimension_semantics=("arbitrary", "arbitrary"),
            vmem_limit_bytes=VMEM_LIMIT),
        name="mixer",
    )(x, mod6, *consts)


def _swiglu_acc(h, w1_ref, w3_ref, w2_ref, acc):
    f = w1_ref.shape[-1]
    for j in range(f // FFN_FC):
        sl = slice(j * FFN_FC, (j + 1) * FFN_FC)
        a = jnp.dot(h, w1_ref[:, sl], preferred_element_type=F32)
        b = jnp.dot(h, w3_ref[:, sl], preferred_element_type=F32)
        p = (a * _sigmoid(a) * b).astype(BF16)
        acc = acc + jnp.dot(p, w2_ref[sl, :], preferred_element_type=F32)
    return acc


def _ffn_kernel(x_ref, mod_ref, ng_ref, w1_ref, w3_ref, w2_ref, *rest):
    n_cast = (len(rest) - 1) // 2
    cast_in, o_ref, cast_out = rest[:n_cast], rest[n_cast], rest[n_cast + 1:]
    x = x_ref[...]
    mod = mod_ref[...]
    h = _rms_mod(x, ng_ref[...], mod[3:4], mod[4:5]).astype(BF16)
    acc = _swiglu_acc(h, w1_ref, w3_ref, w2_ref, jnp.zeros(x.shape, F32))
    o_ref[...] = x + mod[5:6] * acc
    for src, dst in zip(cast_in, cast_out):
        dst[...] = src[...].astype(BF16)


def _ffn(x, mod6, ng, layer, w1, w3, w2, j, cast_src, cast_layer):
    bsz, s, d = x.shape
    f = w1.shape[2]
    tm = FFN_TM
    assert s % tm == 0 and f % FFN_FC == 0
    nblk = s // tm
    steps = bsz * nblk
    slabs = []
    for w in cast_src:
        n_layers, n_exp, rows, cols = w.shape
        assert (n_exp * rows) % steps == 0
        slabs.append(w.reshape(n_layers, steps, n_exp * rows // steps, cols))
    outs = pl.pallas_call(
        _ffn_kernel,
        grid=(bsz, nblk),
        in_specs=[
            pl.BlockSpec((None, tm, d), lambda b, i: (b, i, 0)),
            pl.BlockSpec((None, 6, d), lambda b, i: (b, 0, 0)),
            _layer_spec(ng, layer), _layer_spec(w1, j), _layer_spec(w3, j), _layer_spec(w2, j),
        ] + [pl.BlockSpec((None, None) + w.shape[2:], lambda b, i: (cast_layer, b * nblk + i, 0, 0))
             for w in slabs],
        out_specs=[pl.BlockSpec((None, tm, d), lambda b, i: (b, i, 0))]
        + [pl.BlockSpec((None,) + w.shape[2:], lambda b, i: (b * nblk + i, 0, 0)) for w in slabs],
        out_shape=[jax.ShapeDtypeStruct(x.shape, F32)]
        + [jax.ShapeDtypeStruct(w.shape[1:], BF16) for w in slabs],
        compiler_params=pltpu.CompilerParams(
            dimension_semantics=("arbitrary", "arbitrary"),
            vmem_limit_bytes=FFN_VMEM_LIMIT),
        name="ffn_dense",
    )(x, mod6, ng, w1, w3, w2, *slabs)
    cast = [o.reshape(w.shape[1:]) for o, w in zip(outs[1:], cast_src)]
    return outs[0], cast


def _top2(logits, n_exp):
    lane = lax.broadcasted_iota(jnp.int32, logits.shape, 1).astype(F32)
    neg = jnp.float32(-jnp.inf)
    lg = jnp.where(lane < n_exp, logits, neg)
    m1 = jnp.max(lg, axis=-1, keepdims=True)
    i1 = jnp.min(jnp.where(lg == m1, lane, float(LANES)), axis=-1, keepdims=True)
    lg2 = jnp.where(lane == i1, neg, lg)
    m2 = jnp.max(lg2, axis=-1, keepdims=True)
    i2 = jnp.min(jnp.where(lg2 == m2, lane, float(LANES)), axis=-1, keepdims=True)
    ex = jnp.exp(m2 - m1)
    return i1, i2, 1.0 / (1.0 + ex), ex / (1.0 + ex)


def _router_kernel(x_ref, mod_ref, ng_ref, rwhi_ref, rwlo_ref, h_ref, info_ref, meta_ref, cnt_ref,
                   carry_ref, *, n_exp):
    first = (pl.program_id(0) == 0) & (pl.program_id(1) == 0)

    @pl.when(first)
    def _():
        carry_ref[...] = jnp.zeros_like(carry_ref)

    x = x_ref[...]
    mod = mod_ref[...]
    h = _rms_mod(x, ng_ref[...], mod[3:4], mod[4:5])
    h_ref[...] = h
    h_hi = h.astype(BF16)
    h_lo = (h - h_hi.astype(F32)).astype(BF16)
    logits = (jnp.dot(h_hi, rwhi_ref[...], preferred_element_type=F32)
              + (jnp.dot(h_hi, rwlo_ref[...], preferred_element_type=F32)
                 + jnp.dot(h_lo, rwhi_ref[...], preferred_element_type=F32)))
    i1, i2, p1, p2 = _top2(logits, n_exp)
    tr = x.shape[0]
    lane = lax.broadcasted_iota(jnp.int32, (tr, LANES), 1).astype(F32)
    oh1 = lane == i1
    oh2 = lane == i2
    sel = jnp.where(oh1, 1.0, 0.0) + jnp.where(oh2, 1.0, 0.0)
    r_t = lax.broadcasted_iota(jnp.int32, (tr, tr), 0)
    c_t = lax.broadcasted_iota(jnp.int32, (tr, tr), 1)
    tri = jnp.where(r_t > c_t, 1.0, 0.0).astype(BF16)
    excl = jnp.dot(tri, sel.astype(BF16), preferred_element_type=F32)
    base = carry_ref[0:1] + excl
    rank1 = jnp.sum(jnp.where(oh1, base, 0.0), axis=-1, keepdims=True)
    rank2 = jnp.sum(jnp.where(oh2, base, 0.0), axis=-1, keepdims=True)
    info = jnp.where(lane == 0.0, i1, 0.0)
    for col, val in ((1.0, i2), (2.0, rank1), (3.0, rank2), (4.0, p1), (5.0, p2)):
        info = jnp.where(lane == col, val, info)
    info_ref[...] = info
    meta_ref[...] = jnp.transpose(info)[:SUBLANES]
    carry = carry_ref[...] + jnp.sum(sel, axis=0, keepdims=True)
    carry_ref[...] = carry
    cnt_ref[...] = carry


def _router(x, mod6, ng, layer, rw_p, j, n_exp):
    bsz, s, d = x.shape
    tr = ROUTER_TR
    assert s % tr == 0
    nblk = s // tr
    t = bsz * s
    kern = functools.partial(_router_kernel, n_exp=n_exp)
    return pl.pallas_call(
        kern,
        grid=(bsz, nblk),
        in_specs=[
            pl.BlockSpec((None, tr, d), lambda b, i: (b, i, 0)),
            pl.BlockSpec((None, 6, d), lambda b, i: (b, 0, 0)),
            _layer_spec(ng, layer), _layer_spec(rw_p[0], j), _layer_spec(rw_p[1], j),
        ],
        out_specs=[
            pl.BlockSpec((tr, d), lambda b, i: (b * nblk + i, 0)),
            pl.BlockSpec((tr, LANES), lambda b, i: (b * nblk + i, 0)),
            pl.BlockSpec((SUBLANES, tr), lambda b, i: (0, b * nblk + i)),
            pl.BlockSpec((SUBLANES, LANES), lambda b, i: (0, 0)),
        ],
        out_shape=[
            jax.ShapeDtypeStruct((t, d), F32),
            jax.ShapeDtypeStruct((t, LANES), F32),
            jax.ShapeDtypeStruct((SUBLANES, t), F32),
            jax.ShapeDtypeStruct((SUBLANES, LANES), F32),
        ],
        scratch_shapes=[pltpu.VMEM((SUBLANES, LANES), F32)],
        compiler_params=pltpu.CompilerParams(
            dimension_semantics=("arbitrary", "arbitrary"),
            vmem_limit_bytes=VMEM_LIMIT),
        name="moe_router",
    )(x, mod6, ng, *rw_p)


def _row_gather_start(src_hbm, row, dst, dst_row, sem):
    pltpu.make_async_copy(src_hbm.at[pl.ds(row, 1)], dst.at[pl.ds(dst_row, 1)], sem).start()


def _row_gather_wait(src_hbm, dst, sem):
    pltpu.make_async_copy(src_hbm.at[pl.ds(0, dst.shape[0])], dst, sem).wait()


def _expert_kernel(te_ref, nt_ref, cnt_ref, start_ref, d1_ref, d2_ref, h_hbm, w1_ref, w3_ref,
                   w2_ref, o_ref, xbuf, src_ref, sem):
    i = pl.program_id(0)
    n_tiles = nt_ref[0]
    tm = xbuf.shape[1]
    n_tok = d1_ref.shape[0]

    @pl.when(i == 0)
    def _():
        for e in range(cnt_ref.shape[0]):
            lo = start_ref[e] + cnt_ref[e]
            hi = start_ref[e] + _idiv(cnt_ref[e] + (tm - 1), tm) * tm

            def pad_body(p, carry):
                src_ref[p] = 0
                return carry

            lax.fori_loop(lo, hi, pad_body, 0)

        def inv_body(t, carry):
            src_ref[d1_ref[t]] = t
            src_ref[d2_ref[t]] = t
            return carry

        lax.fori_loop(0, n_tok, inv_body, 0, unroll=8)

        def first_body(r, carry):
            _row_gather_start(h_hbm, src_ref[r], xbuf.at[0], r, sem.at[0])
            return carry

        lax.fori_loop(0, tm, first_body, 0, unroll=8)

    @pl.when(i + 1 < n_tiles)
    def _():
        nxt_slot = (i + 1) % 2
        nxt_base = (i + 1) * tm
        for r in range(tm):
            _row_gather_start(h_hbm, src_ref[nxt_base + r], xbuf.at[nxt_slot], r, sem.at[nxt_slot])

    @pl.when(i < n_tiles)
    def _():
        slot = i % 2
        _row_gather_wait(h_hbm, xbuf.at[slot], sem.at[slot])
        xs = xbuf[slot].astype(BF16)
        o_ref[...] = _swiglu_acc(xs, w1_ref, w3_ref, w2_ref, jnp.zeros(o_ref.shape, F32))

    @pl.when(i >= n_tiles)
    def _():
        o_ref[...] = jnp.zeros_like(o_ref)


def _experts(tile_expert, n_tiles, counts, row_start, dest1, dest2, h, w1, w3, w2, nt_max):
    t, d = h.shape
    f = w1.shape[2]
    tm = MOE_TM
    assert f % FFN_FC == 0

    def w_map(i, te, *_):
        return (te[i], 0, 0)

    grid_spec = pltpu.PrefetchScalarGridSpec(
        num_scalar_prefetch=6,
        grid=(nt_max,),
        in_specs=[
            pl.BlockSpec(memory_space=pl.ANY),
            pl.BlockSpec((None, d, f), w_map),
            pl.BlockSpec((None, d, f), w_map),
            pl.BlockSpec((None, f, d), w_map),
        ],
        out_specs=pl.BlockSpec((tm, d), lambda i, *_: (i, 0)),
        scratch_shapes=[
            pltpu.VMEM((2, tm, d), F32),
            pltpu.SMEM((nt_max * tm,), jnp.int32),
            pltpu.SemaphoreType.DMA((2,)),
        ],
    )
    return pl.pallas_call(
        _expert_kernel,
        grid_spec=grid_spec,
        out_shape=jax.ShapeDtypeStruct((nt_max * tm, d), F32),
        compiler_params=pltpu.CompilerParams(
            dimension_semantics=("arbitrary",),
            vmem_limit_bytes=VMEM_LIMIT),
        name="moe_experts",
    )(tile_expert, n_tiles, counts, row_start, dest1, dest2, h, w1, w3, w2)


def _combine_kernel(d1_ref, d2_ref, x_ref, mod_ref, info_ref, fg_ref, y_hbm, o_ref, ybuf, sem,
                    *, final_norm):
    i = pl.program_id(0)
    n = pl.num_programs(0)
    tc = x_ref.shape[0]

    def issue(tile, slot):
        base = tile * tc
        for r in range(tc):
            _row_gather_start(y_hbm, d1_ref[base + r], ybuf.at[slot, 0], r, sem.at[slot])
            _row_gather_start(y_hbm, d2_ref[base + r], ybuf.at[slot, 1], r, sem.at[slot])

    @pl.when(i == 0)
    def _():
        issue(0, 0)

    @pl.when(i + 1 < n)
    def _():
        issue(i + 1, (i + 1) % 2)

    slot = i % 2
    _row_gather_wait(y_hbm, ybuf.at[slot, 0], sem.at[slot])
    _row_gather_wait(y_hbm, ybuf.at[slot, 1], sem.at[slot])
    info = info_ref[...]
    y = info[:, 4:5] * ybuf[slot, 0] + info[:, 5:6] * ybuf[slot, 1]
    out = x_ref[...] + mod_ref[5:6, :] * y
    if final_norm:
        ms = jnp.mean(out * out, axis=-1, keepdims=True)
        out = out * lax.rsqrt(ms + EPS) * fg_ref[...]
    o_ref[...] = out


def _combine(dest1, dest2, x, mod6, info, final_g, y, final_norm):
    bsz, s, d = x.shape
    tc = ROUTER_TR
    nblk = s // tc
    kern = functools.partial(_combine_kernel, final_norm=final_norm)
    grid_spec = pltpu.PrefetchScalarGridSpec(
        num_scalar_prefetch=2,
        grid=(bsz * nblk,),
        in_specs=[
            pl.BlockSpec((None, tc, d), lambda i, d1, d2: (i // nblk, i % nblk, 0)),
            pl.BlockSpec((None, 6, d), lambda i, d1, d2: (i // nblk, 0, 0)),
            pl.BlockSpec((tc, LANES), lambda i, d1, d2: (i, 0)),
            pl.BlockSpec(final_g.shape, lambda i, d1, d2: (0, 0)),
            pl.BlockSpec(memory_space=pl.ANY),
        ],
        out_specs=pl.BlockSpec((None, tc, d), lambda i, d1, d2: (i // nblk, i % nblk, 0)),
        scratch_shapes=[
            pltpu.VMEM((2, 2, tc, d), F32),
            pltpu.SemaphoreType.DMA((2,)),
        ],
    )
    return pl.pallas_call(
        kern,
        grid_spec=grid_spec,
        out_shape=jax.ShapeDtypeStruct(x.shape, F32),
        compiler_params=pltpu.CompilerParams(
            dimension_semantics=("arbitrary",),
            vmem_limit_bytes=VMEM_LIMIT),
        name="moe_combine",
    )(dest1, dest2, x, mod6, info, final_g, y)


def _moe(x, mod6, ng, layer, rw_p, w1, w3, w2, j, final_g, final_norm):
    bsz, s, d = x.shape
    n_exp = w1.shape[0]
    t = bsz * s
    tm = MOE_TM
    nt_max = (TOP_K * t) // tm + n_exp
    h, info, meta, cnt = _router(x, mod6, ng, layer, rw_p, j, n_exp)

    counts = cnt[0, :n_exp].astype(jnp.int32)
    tiles_per = (counts + tm - 1) // tm
    tile_end = jnp.cumsum(tiles_per)
    row_start = (tile_end - tiles_per) * tm
    n_tiles = tile_end[-1]
    meta = meta.astype(jnp.int32)
    dest1 = row_start[meta[0]] + meta[2]
    dest2 = row_start[meta[1]] + meta[3]
    tile_ids = jnp.arange(nt_max, dtype=jnp.int32)
    live_ids = jnp.minimum(tile_ids, n_tiles - 1)
    tile_expert = jnp.sum((live_ids[:, None] >= tile_end[None, :]).astype(jnp.int32), axis=1)
    tile_expert = jnp.minimum(tile_expert, n_exp - 1)

    y = _experts(tile_expert, n_tiles[None].astype(jnp.int32), counts, row_start, dest1, dest2,
                 h, w1, w3, w2, nt_max)
    return _combine(dest1, dest2, x, mod6, info, final_g, y, final_norm)


def _final_norm_kernel(x_ref, g_ref, o_ref):
    x = x_ref[...]
    ms = jnp.mean(x * x, axis=-1, keepdims=True)
    o_ref[...] = x * lax.rsqrt(ms + EPS) * g_ref[...]


def _final_norm(x, g):
    bsz, s, d = x.shape
    tm = FFN_TM
    assert s % tm == 0
    return pl.pallas_call(
        _final_norm_kernel,
        grid=(bsz, s // tm),
        in_specs=[pl.BlockSpec((None, tm, d), lambda b, i: (b, i, 0)),
                  _const_spec(g.shape)],
        out_specs=pl.BlockSpec((None, tm, d), lambda b, i: (b, i, 0)),
        out_shape=jax.ShapeDtypeStruct(x.shape, F32),
        compiler_params=pltpu.CompilerParams(
            dimension_semantics=("arbitrary", "arbitrary"),
            vmem_limit_bytes=VMEM_LIMIT),
        name="final_norm",
    )(x, g)


def _block_diag(w):
    depth, n, a, b = w.shape
    eye = jnp.eye(n, dtype=w.dtype)
    return jnp.einsum('lnde,nm->lndme', w, eye).reshape(depth, n * a, n * b)


def kernel(x, c, ada_w, ada_b, norm_mix_g, norm_ffn_g, w_in, rg_conv_w, rg_conv_b, rg_wa, rg_ba, rg_wx, rg_bx, rg_lambda, gla_wg2, gla_bg, gla_norm_g, w_out, ffn_w1, ffn_w3, ffn_w2, router_w, moe_w1, moe_w3, moe_w2, final_g):
    bsz, s, d = x.shape
    depth = ada_w.shape[0]
    d_rg = rg_conv_w.shape[2]
    rank = gla_wg2.shape[1]
    n_exp = router_w.shape[2]
    d_main = w_in.shape[2] - rank

    mod = _adaln_mod(c, ada_w, ada_b).reshape(depth, bsz, 6, d)

    win_p = jnp.concatenate(
        [w_in[:, :, :d_main],
         jnp.pad(w_in[:, :, d_main:], ((0, 0), (0, 0), (0, LANES - rank)))], axis=2).astype(BF16)
    wg2_p = jnp.pad(gla_wg2, ((0, 0), (0, LANES - rank), (0, 0))).astype(BF16)
    wgate = jnp.concatenate([_block_diag(rg_wa), _block_diag(rg_wx)], axis=2).astype(BF16)
    bgate = jnp.concatenate([rg_ba, rg_bx], axis=1)[:, None, :]
    mixer_consts = [norm_mix_g[:, None, :], win_p, rg_conv_w, rg_conv_b[:, None, :], wgate, bgate,
                    rg_lambda[:, None, :], wg2_p, gla_bg[:, None, :], gla_norm_g[:, None, :],
                    w_out.astype(BF16)]
    ng_ffn = norm_ffn_g[:, None, :]
    ffn_w = [w.astype(BF16) for w in (ffn_w1, ffn_w3, ffn_w2)]
    moe_w_f32 = (moe_w1, moe_w3, moe_w2)
    moe_w = None
    rw_f32 = jnp.pad(router_w, ((0, 0), (0, 0), (0, LANES - n_exp)))
    rw_hi = rw_f32.astype(BF16)
    rw_p = (rw_hi, (rw_f32 - rw_hi.astype(F32)).astype(BF16))

    for layer in range(depth):
        x = _mixer(x, mod[layer], layer, mixer_consts)
        j = layer // 2
        if layer % 2 == 0:
            cast_src = moe_w_f32 if layer + 1 < depth else ()
            x, moe_w = _ffn(x, mod[layer], ng_ffn, layer, *ffn_w, j, cast_src, j)
        else:
            last = layer == depth - 1
            x = _moe(x, mod[layer], ng_ffn, layer, rw_p, *moe_w, j, final_g[None, :],
                     final_norm=last)
            if last:
                return x
    return _final_norm(x, final_g[None, :])
```

```python
import functools

import jax
import jax.numpy as jnp
from jax import lax
from jax.experimental import pallas as pl
from jax.experimental.pallas import tpu as pltpu

F32 = jnp.float32
BF16 = jnp.bfloat16

EPS = 1e-6
RG_C = 8.0
CONV_W = 4
RG_BLOCKS = 8
GLA_HEADS = 4
GLA_TAU = 16.0
GLA_CHUNK = 64
TOP_K = 2

LANES = 128
SUBLANES = 8
VMEM_LIMIT = 56 * 1024 * 1024
FFN_VMEM_LIMIT = 60 * 1024 * 1024

MIX_TS = 256
MIX_NSEQ = 4
FFN_TM = 512
FFN_FC = 256
ROUTER_TR = 256
MOE_TM = 512


def _sigmoid(x):
    return 1.0 / (1.0 + jnp.exp(-x))


def _rms_mod(x, g, shift, scale):
    ms = jnp.mean(x * x, axis=-1, keepdims=True)
    return (x * lax.rsqrt(ms + EPS)) * (g * (1.0 + scale)) + shift


def _log2(n):
    assert n > 0 and n & (n - 1) == 0, n
    return n.bit_length() - 1


def _idiv(x, n):
    return lax.shift_right_logical(x, _log2(n))


def _imod(x, n):
    _log2(n)
    return x & (n - 1)


def _shift_rows(x, d, fill, seg=None):
    rolled = pltpu.roll(x, d, axis=0)
    row = lax.broadcasted_iota(jnp.int32, x.shape, 0)
    if seg is not None:
        row = _imod(row, seg)
    return jnp.where(row >= d, rolled, fill)


def _linear_scan(a, b, h0):
    rows, n = a.shape
    groups = rows // SUBLANES
    a3 = a.reshape(groups, SUBLANES, n)
    b3 = b.reshape(groups, SUBLANES, n)
    sub = lax.broadcasted_iota(jnp.int32, a3.shape, 1)
    d = 1
    while d < SUBLANES:
        keep = sub >= d
        b3 = a3 * jnp.where(keep, pltpu.roll(b3, d, axis=1), 0.0) + b3
        a3 = a3 * jnp.where(keep, pltpu.roll(a3, d, axis=1), 1.0)
        d *= 2
    carry = h0
    out = []
    for gi in range(groups):
        hg = a3[gi] * carry + b3[gi]
        out.append(hg)
        carry = hg[SUBLANES - 1:SUBLANES]
    return jnp.concatenate(out, axis=0)


def _mod_kernel(c_ref, w_ref, b_ref, o_ref):
    kk = pl.program_id(1)
    c = c_ref[...]
    ca = c * _sigmoid(c)
    part = jnp.dot(ca.astype(BF16), w_ref[...].astype(BF16), preferred_element_type=F32)

    @pl.when(kk == 0)
    def _():
        o_ref[...] = part + b_ref[...]

    @pl.when(kk != 0)
    def _():
        o_ref[...] += part


def _adaln_mod(c, ada_w, ada_b):
    depth, d, n = ada_w.shape
    bsz = c.shape[0]
    rows = -(-bsz // SUBLANES) * SUBLANES
    c_pad = jnp.pad(c, ((0, rows - bsz), (0, 0)))
    tk = 256
    assert d % tk == 0
    out = pl.pallas_call(
        _mod_kernel,
        grid=(depth, d // tk),
        in_specs=[
            pl.BlockSpec((rows, tk), lambda l, k: (0, k)),
            pl.BlockSpec((None, tk, n), lambda l, k: (l, k, 0)),
            pl.BlockSpec((None, 1, n), lambda l, k: (l, 0, 0)),
        ],
        out_specs=pl.BlockSpec((None, rows, n), lambda l, k: (l, 0, 0)),
        out_shape=jax.ShapeDtypeStruct((depth, rows, n), F32),
        compiler_params=pltpu.CompilerParams(
            dimension_semantics=("arbitrary", "arbitrary"),
            vmem_limit_bytes=VMEM_LIMIT),
        name="adaln_mod",
    )(c_pad, ada_w, ada_b.reshape(depth, 1, n))
    return out[:, :bsz, :]


def _mixer_kernel(x_ref, mod_ref, *rest, **dims):
    consts, (o_ref, tail_ref, hst_ref, gst_ref) = rest[:-4], rest[-4:]
    s_idx = pl.program_id(1)

    @pl.when(s_idx == 0)
    def _():
        tail_ref[...] = jnp.zeros_like(tail_ref)
        hst_ref[...] = jnp.zeros_like(hst_ref)
        gst_ref[...] = jnp.zeros_like(gst_ref)

    ng_ref, win_ref = consts[0], consts[1]
    nseq = x_ref.shape[0]
    widths = (dims["d_rg"], dims["d_rg"], dims["kd"], dims["kd"], dims["vd"], dims["vd"], LANES)
    proj = [[] for _ in range(nseq)]
    for k in range(nseq + 1):
        work = []
        if k < nseq:
            work.append((_project_in(x_ref.at[k], mod_ref.at[k], ng_ref, win_ref, widths, proj[k]), 1))
        if k >= 1:
            j = k - 1
            work.append((_mixer_seq(proj[j], x_ref.at[j], mod_ref.at[j], *consts[2:], o_ref.at[j],
                                    tail_ref.at[j], hst_ref.at[j], gst_ref.at[j], s_idx, **dims), 1))
        _alternate(work)


def _alternate(work):
    work = list(work)
    while work:
        for item in list(work):
            gen, n = item
            for _ in range(n):
                try:
                    next(gen)
                except StopIteration:
                    work.remove(item)
                    break


def _project_in(x_ref, mod_ref, ng_ref, win_ref, widths, out):
    mod = mod_ref[...]
    h = _rms_mod(x_ref[...], ng_ref[...], mod[0:1], mod[1:2]).astype(BF16)
    yield
    o0 = 0
    for w in widths:
        parts = []
        for c in range(0, w, 256):
            cw = min(256, w - c)
            parts.append(jnp.dot(h, win_ref[:, o0 + c:o0 + c + cw], preferred_element_type=F32))
            yield
        out.append(parts[0] if len(parts) == 1 else jnp.concatenate(parts, axis=1))
        o0 += w


def _mixer_seq(proj, x_ref, mod_ref, cw_ref, cb_ref, wg_ref, bgate_ref,
               lam_ref, wg2_ref, bg_ref, gg_ref, wout_ref, o_ref,
               tail_ref, hst_ref, gst_ref, s_idx, *, d_rg, kd, vd):
    ts = x_ref.shape[0]
    dk = kd // GLA_HEADS
    dv = vd // GLA_HEADS
    rg_x, rg_gate, q, k, v, g, f_low = proj

    cat = jnp.concatenate([tail_ref[...], rg_x], axis=0)
    cw = cw_ref[...]
    u = cb_ref[...] + cw[CONV_W - 1:CONV_W] * rg_x
    for tap in range(CONV_W - 1):
        sh = CONV_W - 1 - tap
        u = u + cw[tap:tap + 1] * pltpu.roll(cat, sh, axis=0)[SUBLANES:SUBLANES + ts]
    tail_ref[...] = rg_x[ts - SUBLANES:ts]
    yield

    gates = jnp.dot(u.astype(BF16), wg_ref[...], preferred_element_type=F32) + bgate_ref[...]
    yield
    lam = lam_ref[...]
    sp = jnp.maximum(-lam, 0.0) + jnp.log1p(jnp.exp(-jnp.abs(lam)))
    first_row = (lax.broadcasted_iota(jnp.int32, (ts, LANES), 0) == 0) & (s_idx == 0)
    rg_cols = []
    for c0 in range(0, d_rg, LANES):
        cs = slice(c0, c0 + LANES)
        r = _sigmoid(gates[:, cs])
        i_gate = _sigmoid(gates[:, d_rg + c0:d_rg + c0 + LANES])
        log_a = -RG_C * r * sp[:, cs]
        a = jnp.exp(log_a)
        mult = jnp.sqrt(-jnp.tanh(log_a) * (a * a + 1.0))
        mult = jnp.where(first_row, 1.0, mult)
        b_term = mult * i_gate * u[:, cs]
        h_rg = _linear_scan(a, b_term, hst_ref[0:1, cs])
        hst_ref[:, cs] = jnp.broadcast_to(h_rg[ts - 1:ts], (SUBLANES, LANES))
        gate = rg_gate[:, cs]
        gelu = 0.5 * gate * (1.0 + jnp.tanh(0.7978845608028654 * (gate + 0.044715 * gate ** 3)))
        rg_cols.append(h_rg * gelu)
        yield
    rg_out = jnp.concatenate(rg_cols, axis=1)

    fz = jnp.dot(f_low.astype(BF16), wg2_ref[...], preferred_element_type=F32) + bg_ref[...]
    log_f = (jnp.minimum(fz, 0.0) - jnp.log1p(jnp.exp(-jnp.abs(fz)))) * (1.0 / GLA_TAU)
    bcum = log_f
    d = 1
    while d < GLA_CHUNK:
        bcum = bcum + _shift_rows(bcum, d, 0.0, seg=GLA_CHUNK)
        d *= 2

    yield
    scale = dk ** -0.5
    nh = GLA_HEADS
    c_len = GLA_CHUNK
    r_i = lax.broadcasted_iota(jnp.int32, (nh * c_len, kd), 0)
    c_i = lax.broadcasted_iota(jnp.int32, (nh * c_len, kd), 1)
    head_mask = _idiv(r_i, c_len) == _idiv(c_i, dk)
    r_j = lax.broadcasted_iota(jnp.int32, (nh * c_len, c_len), 0)
    c_j = lax.broadcasted_iota(jnp.int32, (nh * c_len, c_len), 1)
    causal = c_j <= _imod(r_j, c_len)
    r_s = lax.broadcasted_iota(jnp.int32, (vd, kd), 0)
    c_s = lax.broadcasted_iota(jnp.int32, (vd, kd), 1)
    bd_mask = _idiv(r_s, dv) == _idiv(c_s, dk)

    state = gst_ref[...]
    o_chunks = []
    for ci in range(ts // c_len):
        lo = ci * c_len
        bc = bcum[lo:lo + c_len]
        qc = q[lo:lo + c_len] * scale
        kc = k[lo:lo + c_len]
        vc = v[lo:lo + c_len].astype(BF16)
        b_ref_row = bc[c_len // 2 - 1:c_len // 2]
        b_last = bc[c_len - 1:c_len]
        q_loc = (qc * jnp.exp(bc - b_ref_row)).astype(BF16)
        k_loc = (kc * jnp.exp(b_ref_row - bc)).astype(BF16)
        q_stack = jnp.where(head_mask, jnp.concatenate([q_loc] * nh, axis=0), jnp.zeros((), BF16))
        scores = lax.dot_general(q_stack, k_loc, (((1,), (1,)), ((), ())),
                                 preferred_element_type=F32)
        p = jnp.where(causal, scores, 0.0).astype(BF16)
        oi = jnp.dot(p, vc, preferred_element_type=F32)
        o_intra = jnp.concatenate(
            [oi[hh * c_len:(hh + 1) * c_len, hh * dv:(hh + 1) * dv] for hh in range(nh)], axis=1)
        q_b = (qc * jnp.exp(bc)).astype(BF16)
        o_inter = lax.dot_general(q_b, state.astype(BF16), (((1,), (1,)), ((), ())),
                                  preferred_element_type=F32)
        o_chunks.append(o_intra + o_inter)
        k_end = (kc * jnp.exp(b_last - bc)).astype(BF16)
        upd = lax.dot_general(vc, k_end, (((0,), (0,)), ((), ())),
                              preferred_element_type=F32)
        state = jnp.exp(b_last) * state + jnp.where(bd_mask, upd, 0.0)
        yield
    gst_ref[...] = state
    o = jnp.concatenate(o_chunks, axis=0)

    gg = gg_ref[...]
    heads = []
    for hh in range(nh):
        hs = slice(hh * dv, (hh + 1) * dv)
        oh = o[:, hs]
        ms = jnp.mean(oh * oh, axis=-1, keepdims=True)
        gh = g[:, hs]
        heads.append((oh * lax.rsqrt(ms + EPS) * gg) * (gh * _sigmoid(gh)))
        yield
    gla_out = jnp.concatenate(heads, axis=1)

    mix_in = jnp.concatenate([rg_out, gla_out], axis=1).astype(BF16)
    mix = jnp.dot(mix_in, wout_ref[...], preferred_element_type=F32)
    o_ref[...] = x_ref[...] + mod_ref[2:3, :] * mix


def _const_spec(shape):
    nd = len(shape)
    return pl.BlockSpec(shape, lambda *_: (0,) * nd, pipeline_mode=pl.Buffered(1))


def _layer_spec(stacked, layer):
    rest = stacked.shape[1:]
    return pl.BlockSpec((None,) + rest, lambda *_: (layer,) + (0,) * len(rest),
                        pipeline_mode=pl.Buffered(1))


def _mixer(x, mod6, layer, consts):
    bsz, s, d = x.shape
    d_rg = consts[2].shape[2]
    kd = consts[8].shape[2]
    vd = consts[10].shape[1] - d_rg
    ts = MIX_TS
    nseq = MIX_NSEQ if bsz % MIX_NSEQ == 0 else 1
    assert s % ts == 0 and ts % GLA_CHUNK == 0
    kern = functools.partial(_mixer_kernel, d_rg=d_rg, kd=kd, vd=vd)
    return pl.pallas_call(
        kern,
        grid=(bsz // nseq, s // ts),
        in_specs=[
            pl.BlockSpec((nseq, ts, d), lambda b, i: (b, i, 0)),
            pl.BlockSpec((nseq, 6, d), lambda b, i: (b, 0, 0)),
        ] + [_layer_spec(a, layer) for a in consts],
        out_specs=pl.BlockSpec((nseq, ts, d), lambda b, i: (b, i, 0)),
        out_shape=jax.ShapeDtypeStruct(x.shape, F32),
        scratch_shapes=[
            pltpu.VMEM((nseq, SUBLANES, d_rg), F32),
            pltpu.VMEM((nseq, SUBLANES, d_rg), F32),
            pltpu.VMEM((nseq, vd, kd), F32),
        ],
        compiler_params=pltpu.CompilerParams(
            dimension_semantics=("arbitrary", "arbitrary"),
            vmem_limit_bytes=VMEM_LIMIT),
        name="mixer",
    )(x, mod6, *consts)


def _swiglu_acc(h, w1_ref, w3_ref, w2_ref, acc):
    f = w1_ref.shape[-1]
    for j in range(f // FFN_FC):
        sl = slice(j * FFN_FC, (j + 1) * FFN_FC)
        a = jnp.dot(h, w1_ref[:, sl], preferred_element_type=F32)
        b = jnp.dot(h, w3_ref[:, sl], preferred_element_type=F32)
        p = (a * _sigmoid(a) * b).astype(BF16)
        acc = acc + jnp.dot(p, w2_ref[sl, :], preferred_element_type=F32)
    return acc


def _ffn_kernel(x_ref, mod_ref, ng_ref, w1_ref, w3_ref, w2_ref, *rest):
    n_cast = (len(rest) - 1) // 2
    cast_in, o_ref, cast_out = rest[:n_cast], rest[n_cast], rest[n_cast + 1:]
    x = x_ref[...]
    mod = mod_ref[...]
    h = _rms_mod(x, ng_ref[...], mod[3:4], mod[4:5]).astype(BF16)
    acc = _swiglu_acc(h, w1_ref, w3_ref, w2_ref, jnp.zeros(x.shape, F32))
    o_ref[...] = x + mod[5:6] * acc
    for src, dst in zip(cast_in, cast_out):
        dst[...] = src[...].astype(BF16)


def _ffn(x, mod6, ng, layer, w1, w3, w2, j, cast_src, cast_layer):
    bsz, s, d = x.shape
    f = w1.shape[2]
    tm = FFN_TM
    assert s % tm == 0 and f % FFN_FC == 0
    nblk = s // tm
    steps = bsz * nblk
    slabs = []
    for w in cast_src:
        n_layers, n_exp, rows, cols = w.shape
        assert (n_exp * rows) % steps == 0
        slabs.append(w.reshape(n_layers, steps, n_exp * rows // steps, cols))
    outs = pl.pallas_call(
        _ffn_kernel,
        grid=(bsz, nblk),
        in_specs=[
            pl.BlockSpec((None, tm, d), lambda b, i: (b, i, 0)),
            pl.BlockSpec((None, 6, d), lambda b, i: (b, 0, 0)),
            _layer_spec(ng, layer), _layer_spec(w1, j), _layer_spec(w3, j), _layer_spec(w2, j),
        ] + [pl.BlockSpec((None, None) + w.shape[2:], lambda b, i: (cast_layer, b * nblk + i, 0, 0))
             for w in slabs],
        out_specs=[pl.BlockSpec((None, tm, d), lambda b, i: (b, i, 0))]
        + [pl.BlockSpec((None,) + w.shape[2:], lambda b, i: (b * nblk + i, 0, 0)) for w in slabs],
        out_shape=[jax.ShapeDtypeStruct(x.shape, F32)]
        + [jax.ShapeDtypeStruct(w.shape[1:], BF16) for w in slabs],
        compiler_params=pltpu.CompilerParams(
            dimension_semantics=("arbitrary", "arbitrary"),
            vmem_limit_bytes=FFN_VMEM_LIMIT),
        name="ffn_dense",
    )(x, mod6, ng, w1, w3, w2, *slabs)
    cast = [o.reshape(w.shape[1:]) for o, w in zip(outs[1:], cast_src)]
    return outs[0], cast


def _top2(logits, n_exp):
    lane = lax.broadcasted_iota(jnp.int32, logits.shape, 1).astype(F32)
    neg = jnp.float32(-jnp.inf)
    lg = jnp.where(lane < n_exp, logits, neg)
    m1 = jnp.max(lg, axis=-1, keepdims=True)
    i1 = jnp.min(jnp.where(lg == m1, lane, float(LANES)), axis=-1, keepdims=True)
    lg2 = jnp.where(lane == i1, neg, lg)
    m2 = jnp.max(lg2, axis=-1, keepdims=True)
    i2 = jnp.min(jnp.where(lg2 == m2, lane, float(LANES)), axis=-1, keepdims=True)
    ex = jnp.exp(m2 - m1)
    return i1, i2, 1.0 / (1.0 + ex), ex / (1.0 + ex)


def _router_kernel(x_ref, mod_ref, ng_ref, rwhi_ref, rwlo_ref, h_ref, info_ref, meta_ref, cnt_ref,
                   carry_ref, *, n_exp):
    first = (pl.program_id(0) == 0) & (pl.program_id(1) == 0)

    @pl.when(first)
    def _():
        carry_ref[...] = jnp.zeros_like(carry_ref)

    x = x_ref[...]
    mod = mod_ref[...]
    h = _rms_mod(x, ng_ref[...], mod[3:4], mod[4:5])
    h_ref[...] = h
    h_hi = h.astype(BF16)
    h_lo = (h - h_hi.astype(F32)).astype(BF16)
    logits = (jnp.dot(h_hi, rwhi_ref[...], preferred_element_type=F32)
              + (jnp.dot(h_hi, rwlo_ref[...], preferred_element_type=F32)
                 + jnp.dot(h_lo, rwhi_ref[...], preferred_element_type=F32)))
    i1, i2, p1, p2 = _top2(logits, n_exp)
    tr = x.shape[0]
    lane = lax.broadcasted_iota(jnp.int32, (tr, LANES), 1).astype(F32)
    oh1 = lane == i1
    oh2 = lane == i2
    sel = jnp.where(oh1, 1.0, 0.0) + jnp.where(oh2, 1.0, 0.0)
    r_t = lax.broadcasted_iota(jnp.int32, (tr, tr), 0)
    c_t = lax.broadcasted_iota(jnp.int32, (tr, tr), 1)
    tri = jnp.where(r_t > c_t, 1.0, 0.0).astype(BF16)
    excl = jnp.dot(tri, sel.astype(BF16), preferred_element_type=F32)
    base = carry_ref[0:1] + excl
    rank1 = jnp.sum(jnp.where(oh1, base, 0.0), axis=-1, keepdims=True)
    rank2 = jnp.sum(jnp.where(oh2, base, 0.0), axis=-1, keepdims=True)
    info = jnp.where(lane == 0.0, i1, 0.0)
    for col, val in ((1.0, i2), (2.0, rank1), (3.0, rank2), (4.0, p1), (5.0, p2)):
        info = jnp.where(lane == col, val, info)
    info_ref[...] = info
    meta_ref[...] = jnp.transpose(info)[:SUBLANES]
    carry = carry_ref[...] + jnp.sum(sel, axis=0, keepdims=True)
    carry_ref[...] = carry
    cnt_ref[...] = carry


def _router(x, mod6, ng, layer, rw_p, j, n_exp):
    bsz, s, d = x.shape
    tr = ROUTER_TR
    assert s % tr == 0
    nblk = s // tr
    t = bsz * s
    kern = functools.partial(_router_kernel, n_exp=n_exp)
    return pl.pallas_call(
        kern,
        grid=(bsz, nblk),
        in_specs=[
            pl.BlockSpec((None, tr, d), lambda b, i: (b, i, 0)),
            pl.BlockSpec((None, 6, d), lambda b, i: (b, 0, 0)),
            _layer_spec(ng, layer), _layer_spec(rw_p[0], j), _layer_spec(rw_p[1], j),
        ],
        out_specs=[
            pl.BlockSpec((tr, d), lambda b, i: (b * nblk + i, 0)),
            pl.BlockSpec((tr, LANES), lambda b, i: (b * nblk + i, 0)),
            pl.BlockSpec((SUBLANES, tr), lambda b, i: (0, b * nblk + i)),
            pl.BlockSpec((SUBLANES, LANES), lambda b, i: (0, 0)),
        ],
        out_shape=[
            jax.ShapeDtypeStruct((t, d), F32),
            jax.ShapeDtypeStruct((t, LANES), F32),
            jax.ShapeDtypeStruct((SUBLANES, t), F32),
            jax.ShapeDtypeStruct((SUBLANES, LANES), F32),
        ],
        scratch_shapes=[pltpu.VMEM((SUBLANES, LANES), F32)],
        compiler_params=pltpu.CompilerParams(
            dimension_semantics=("arbitrary", "arbitrary"),
            vmem_limit_bytes=VMEM_LIMIT),
        name="moe_router",
    )(x, mod6, ng, *rw_p)


def _row_gather_start(src_hbm, row, dst, dst_row, sem):
    pltpu.make_async_copy(src_hbm.at[pl.ds(row, 1)], dst.at[pl.ds(dst_row, 1)], sem).start()


def _row_gather_wait(src_hbm, dst, sem):
    pltpu.make_async_copy(src_hbm.at[pl.ds(0, dst.shape[0])], dst, sem).wait()


def _expert_kernel(te_ref, nt_ref, cnt_ref, start_ref, d1_ref, d2_ref, h_hbm, w1_ref, w3_ref,
                   w2_ref, o_ref, xbuf, src_ref, sem):
    i = pl.program_id(0)
    n_tiles = nt_ref[0]
    tm = xbuf.shape[1]
    n_tok = d1_ref.shape[0]

    @pl.when(i == 0)
    def _():
        for e in range(cnt_ref.shape[0]):
            lo = start_ref[e] + cnt_ref[e]
            hi = start_ref[e] + _idiv(cnt_ref[e] + (tm - 1), tm) * tm

            def pad_body(p, carry):
                src_ref[p] = 0
                return carry

            lax.fori_loop(lo, hi, pad_body, 0)

        def inv_body(t, carry):
            src_ref[d1_ref[t]] = t
            src_ref[d2_ref[t]] = t
            return carry

        lax.fori_loop(0, n_tok, inv_body, 0, unroll=8)

        def first_body(r, carry):
            _row_gather_start(h_hbm, src_ref[r], xbuf.at[0], r, sem.at[0])
            return carry

        lax.fori_loop(0, tm, first_body, 0, unroll=8)

    @pl.when(i + 1 < n_tiles)
    def _():
        nxt_slot = (i + 1) % 2
        nxt_base = (i + 1) * tm
        for r in range(tm):
            _row_gather_start(h_hbm, src_ref[nxt_base + r], xbuf.at[nxt_slot], r, sem.at[nxt_slot])

    @pl.when(i < n_tiles)
    def _():
        slot = i % 2
        _row_gather_wait(h_hbm, xbuf.at[slot], sem.at[slot])
        xs = xbuf[slot].astype(BF16)
        o_ref[...] = _swiglu_acc(xs, w1_ref, w3_ref, w2_ref, jnp.zeros(o_ref.shape, F32))

    @pl.when(i >= n_tiles)
    def _():
        o_ref[...] = jnp.zeros_like(o_ref)


def _experts(tile_expert, n_tiles, counts, row_start, dest1, dest2, h, w1, w3, w2, nt_max):
    t, d = h.shape
    f = w1.shape[2]
    tm = MOE_TM
    assert f % FFN_FC == 0

    def w_map(i, te, *_):
        return (te[i], 0, 0)

    grid_spec = pltpu.PrefetchScalarGridSpec(
        num_scalar_prefetch=6,
        grid=(nt_max,),
        in_specs=[
            pl.BlockSpec(memory_space=pl.ANY),
            pl.BlockSpec((None, d, f), w_map),
            pl.BlockSpec((None, d, f), w_map),
            pl.BlockSpec((None, f, d), w_map),
        ],
        out_specs=pl.BlockSpec((tm, d), lambda i, *_: (i, 0)),
        scratch_shapes=[
            pltpu.VMEM((2, tm, d), F32),
            pltpu.SMEM((nt_max * tm,), jnp.int32),
            pltpu.SemaphoreType.DMA((2,)),
        ],
    )
    return pl.pallas_call(
        _expert_kernel,
        grid_spec=grid_spec,
        out_shape=jax.ShapeDtypeStruct((nt_max * tm, d), F32),
        compiler_params=pltpu.CompilerParams(
            dimension_semantics=("arbitrary",),
            vmem_limit_bytes=VMEM_LIMIT),
        name="moe_experts",
    )(tile_expert, n_tiles, counts, row_start, dest1, dest2, h, w1, w3, w2)


def _combine_kernel(d1_ref, d2_ref, x_ref, mod_ref, info_ref, fg_ref, y_hbm, o_ref, ybuf, sem,
                    *, final_norm):
    i = pl.program_id(0)
    n = pl.num_programs(0)
    tc = x_ref.shape[0]

    def issue(tile, slot):
        base = tile * tc
        for r in range(tc):
            _row_gather_start(y_hbm, d1_ref[base + r], ybuf.at[slot, 0], r, sem.at[slot])
            _row_gather_start(y_hbm, d2_ref[base + r], ybuf.at[slot, 1], r, sem.at[slot])

    @pl.when(i == 0)
    def _():
        issue(0, 0)

    @pl.when(i + 1 < n)
    def _():
        issue(i + 1, (i + 1) % 2)

    slot = i % 2
    _row_gather_wait(y_hbm, ybuf.at[slot, 0], sem.at[slot])
    _row_gather_wait(y_hbm, ybuf.at[slot, 1], sem.at[slot])
    info = info_ref[...]
    y = info[:, 4:5] * ybuf[slot, 0] + info[:, 5:6] * ybuf[slot, 1]
    out = x_ref[...] + mod_ref[5:6, :] * y
    if final_norm:
        ms = jnp.mean(out * out, axis=-1, keepdims=True)
        out = out * lax.rsqrt(ms + EPS) * fg_ref[...]
    o_ref[...] = out


def _combine(dest1, dest2, x, mod6, info, final_g, y, final_norm):
    bsz, s, d = x.shape
    tc = ROUTER_TR
    nblk = s // tc
    kern = functools.partial(_combine_kernel, final_norm=final_norm)
    grid_spec = pltpu.PrefetchScalarGridSpec(
        num_scalar_prefetch=2,
        grid=(bsz * nblk,),
        in_specs=[
            pl.BlockSpec((None, tc, d), lambda i, d1, d2: (i // nblk, i % nblk, 0)),
            pl.BlockSpec((None, 6, d), lambda i, d1, d2: (i // nblk, 0, 0)),
            pl.BlockSpec((tc, LANES), lambda i, d1, d2: (i, 0)),
            pl.BlockSpec(final_g.shape, lambda i, d1, d2: (0, 0)),
            pl.BlockSpec(memory_space=pl.ANY),
        ],
        out_specs=pl.BlockSpec((None, tc, d), lambda i, d1, d2: (i // nblk, i % nblk, 0)),
        scratch_shapes=[
            pltpu.VMEM((2, 2, tc, d), F32),
            pltpu.SemaphoreType.DMA((2,)),
        ],
    )
    return pl.pallas_call(
        kern,
        grid_spec=grid_spec,
        out_shape=jax.ShapeDtypeStruct(x.shape, F32),
        compiler_params=pltpu.CompilerParams(
            dimension_semantics=("arbitrary",),
            vmem_limit_bytes=VMEM_LIMIT),
        name="moe_combine",
    )(dest1, dest2, x, mod6, info, final_g, y)


def _moe(x, mod6, ng, layer, rw_p, w1, w3, w2, j, final_g, final_norm):
    bsz, s, d = x.shape
    n_exp = w1.shape[0]
    t = bsz * s
    tm = MOE_TM
    nt_max = (TOP_K * t) // tm + n_exp
    h, info, meta, cnt = _router(x, mod6, ng, layer, rw_p, j, n_exp)

    counts = cnt[0, :n_exp].astype(jnp.int32)
    tiles_per = (counts + tm - 1) // tm
    tile_end = jnp.cumsum(tiles_per)
    row_start = (tile_end - tiles_per) * tm
    n_tiles = tile_end[-1]
    meta = meta.astype(jnp.int32)
    dest1 = row_start[meta[0]] + meta[2]
    dest2 = row_start[meta[1]] + meta[3]
    tile_ids = jnp.arange(nt_max, dtype=jnp.int32)
    live_ids = jnp.minimum(tile_ids, n_tiles - 1)
    tile_expert = jnp.sum((live_ids[:, None] >= tile_end[None, :]).astype(jnp.int32), axis=1)
    tile_expert = jnp.minimum(tile_expert, n_exp - 1)

    y = _experts(tile_expert, n_tiles[None].astype(jnp.int32), counts, row_start, dest1, dest2,
                 h, w1, w3, w2, nt_max)
    return _combine(dest1, dest2, x, mod6, info, final_g, y, final_norm)


def _final_norm_kernel(x_ref, g_ref, o_ref):
    x = x_ref[...]
    ms = jnp.mean(x * x, axis=-1, keepdims=True)
    o_ref[...] = x * lax.rsqrt(ms + EPS) * g_ref[...]


def _final_norm(x, g):
    bsz, s, d = x.shape
    tm = FFN_TM
    assert s % tm == 0
    return pl.pallas_call(
        _final_norm_kernel,
        grid=(bsz, s // tm),
        in_specs=[pl.BlockSpec((None, tm, d), lambda b, i: (b, i, 0)),
                  _const_spec(g.shape)],
        out_specs=pl.BlockSpec((None, tm, d), lambda b, i: (b, i, 0)),
        out_shape=jax.ShapeDtypeStruct(x.shape, F32),
        compiler_params=pltpu.CompilerParams(
            dimension_semantics=("arbitrary", "arbitrary"),
            vmem_limit_bytes=VMEM_LIMIT),
        name="final_norm",
    )(x, g)


def _block_diag(w):
    depth, n, a, b = w.shape
    eye = jnp.eye(n, dtype=w.dtype)
    return jnp.einsum('lnde,nm->lndme', w, eye).reshape(depth, n * a, n * b)


def kernel(x, c, ada_w, ada_b, norm_mix_g, norm_ffn_g, w_in, rg_conv_w, rg_conv_b, rg_wa, rg_ba, rg_wx, rg_bx, rg_lambda, gla_wg2, gla_bg, gla_norm_g, w_out, ffn_w1, ffn_w3, ffn_w2, router_w, moe_w1, moe_w3, moe_w2, final_g):
    bsz, s, d = x.shape
    depth = ada_w.shape[0]
    d_rg = rg_conv_w.shape[2]
    rank = gla_wg2.shape[1]
    n_exp = router_w.shape[2]
    d_main = w_in.shape[2] - rank

    mod = _adaln_mod(c, ada_w, ada_b).reshape(depth, bsz, 6, d)

    assert d_main % LANES == 0
    win_p = jnp.pad(w_in, ((0, 0), (0, 0), (0, LANES - rank))).astype(BF16)
    wg2_p = jnp.pad(gla_wg2, ((0, 0), (0, LANES - rank), (0, 0))).astype(BF16)
    wgate = jnp.concatenate([_block_diag(rg_wa), _block_diag(rg_wx)], axis=2).astype(BF16)
    bgate = jnp.concatenate([rg_ba, rg_bx], axis=1)[:, None, :]
    mixer_consts = [norm_mix_g[:, None, :], win_p, rg_conv_w, rg_conv_b[:, None, :], wgate, bgate,
                    rg_lambda[:, None, :], wg2_p, gla_bg[:, None, :], gla_norm_g[:, None, :],
                    w_out.astype(BF16)]
    ng_ffn = norm_ffn_g[:, None, :]
    ffn_w = [w.astype(BF16) for w in (ffn_w1, ffn_w3, ffn_w2)]
    moe_w_f32 = (moe_w1, moe_w3, moe_w2)
    moe_w = None
    rw_f32 = jnp.pad(router_w, ((0, 0), (0, 0), (0, LANES - n_exp)))
    rw_hi = rw_f32.astype(BF16)
    rw_p = (rw_hi, (rw_f32 - rw_hi.astype(F32)).astype(BF16))

    for layer in range(depth):
        x = _mixer(x, mod[layer], layer, mixer_consts)
        j = layer // 2
        if layer % 2 == 0:
            cast_src = moe_w_f32 if layer + 1 < depth else ()
            x, moe_w = _ffn(x, mod[layer], ng_ffn, layer, *ffn_w, j, cast_src, j)
        else:
            last = layer == depth - 1
            x = _moe(x, mod[layer], ng_ffn, layer, rw_p, *moe_w, j, final_g[None, :],
                     final_norm=last)
            if last:
                return x
    return _final_norm(x, final_g[None, :])
```

```python
import functools

import jax
import jax.numpy as jnp
from jax import lax
from jax.experimental import pallas as pl
from jax.experimental.pallas import tpu as pltpu

F32 = jnp.float32
BF16 = jnp.bfloat16

EPS = 1e-6
RG_C = 8.0
CONV_W = 4
RG_BLOCKS = 8
GLA_HEADS = 4
GLA_TAU = 16.0
GLA_CHUNK = 64
TOP_K = 2

LANES = 128
SUBLANES = 8
VMEM_LIMIT = 56 * 1024 * 1024
FFN_VMEM_LIMIT = 60 * 1024 * 1024

MIX_TS = 256
MIX_NSEQ = 4
FFN_TM = 512
FFN_FC = 256
ROUTER_TR = 512
MOE_TM = 512


def _sigmoid(x):
    return 1.0 / (1.0 + jnp.exp(-x))


def _rms_mod(x, g, shift, scale):
    ms = jnp.mean(x * x, axis=-1, keepdims=True)
    return (x * lax.rsqrt(ms + EPS)) * (g * (1.0 + scale)) + shift


def _log2(n):
    assert n > 0 and n & (n - 1) == 0, n
    return n.bit_length() - 1


def _idiv(x, n):
    return lax.shift_right_logical(x, _log2(n))


def _imod(x, n):
    _log2(n)
    return x & (n - 1)


def _shift_rows(x, d, fill, seg=None):
    rolled = pltpu.roll(x, d, axis=0)
    row = lax.broadcasted_iota(jnp.int32, x.shape, 0)
    if seg is not None:
        row = _imod(row, seg)
    return jnp.where(row >= d, rolled, fill)


def _linear_scan(a, b, h0):
    rows, n = a.shape
    groups = rows // SUBLANES
    a3 = a.reshape(groups, SUBLANES, n)
    b3 = b.reshape(groups, SUBLANES, n)
    sub = lax.broadcasted_iota(jnp.int32, a3.shape, 1)
    d = 1
    while d < SUBLANES:
        keep = sub >= d
        b3 = a3 * jnp.where(keep, pltpu.roll(b3, d, axis=1), 0.0) + b3
        a3 = a3 * jnp.where(keep, pltpu.roll(a3, d, axis=1), 1.0)
        d *= 2
    carry = h0
    out = []
    for gi in range(groups):
        hg = a3[gi] * carry + b3[gi]
        out.append(hg)
        carry = hg[SUBLANES - 1:SUBLANES]
    return jnp.concatenate(out, axis=0)


def _mod_kernel(c_ref, w_ref, b_ref, o_ref):
    kk = pl.program_id(1)
    c = c_ref[...]
    ca = c * _sigmoid(c)
    part = jnp.dot(ca.astype(BF16), w_ref[...].astype(BF16), preferred_element_type=F32)

    @pl.when(kk == 0)
    def _():
        o_ref[...] = part + b_ref[...]

    @pl.when(kk != 0)
    def _():
        o_ref[...] += part


def _adaln_mod(c, ada_w, ada_b):
    depth, d, n = ada_w.shape
    bsz = c.shape[0]
    rows = -(-bsz // SUBLANES) * SUBLANES
    c_pad = jnp.pad(c, ((0, rows - bsz), (0, 0)))
    tk = 256
    assert d % tk == 0
    out = pl.pallas_call(
        _mod_kernel,
        grid=(depth, d // tk),
        in_specs=[
            pl.BlockSpec((rows, tk), lambda l, k: (0, k)),
            pl.BlockSpec((None, tk, n), lambda l, k: (l, k, 0)),
            pl.BlockSpec((None, 1, n), lambda l, k: (l, 0, 0)),
        ],
        out_specs=pl.BlockSpec((None, rows, n), lambda l, k: (l, 0, 0)),
        out_shape=jax.ShapeDtypeStruct((depth, rows, n), F32),
        compiler_params=pltpu.CompilerParams(
            dimension_semantics=("arbitrary", "arbitrary"),
            vmem_limit_bytes=VMEM_LIMIT),
        name="adaln_mod",
    )(c_pad, ada_w, ada_b.reshape(depth, 1, n))
    return out[:, :bsz, :]


def _mixer_kernel(x_ref, mod_ref, *rest, **dims):
    consts, (o_ref, tail_ref, hst_ref, gst_ref) = rest[:-4], rest[-4:]
    s_idx = pl.program_id(1)

    @pl.when(s_idx == 0)
    def _():
        tail_ref[...] = jnp.zeros_like(tail_ref)
        hst_ref[...] = jnp.zeros_like(hst_ref)
        gst_ref[...] = jnp.zeros_like(gst_ref)

    ng_ref, win_ref = consts[0], consts[1]
    nseq = x_ref.shape[0]
    widths = (dims["d_rg"], dims["d_rg"], dims["kd"], dims["kd"], dims["vd"], dims["vd"], LANES)
    proj = [[] for _ in range(nseq)]
    for k in range(nseq + 1):
        work = []
        if k < nseq:
            work.append((_project_in(x_ref.at[k], mod_ref.at[k], ng_ref, win_ref, widths, proj[k]), 1))
        if k >= 1:
            j = k - 1
            work.append((_mixer_seq(proj[j], x_ref.at[j], mod_ref.at[j], *consts[2:], o_ref.at[j],
                                    tail_ref.at[j], hst_ref.at[j], gst_ref.at[j], s_idx, **dims), 1))
        _alternate(work)


def _alternate(work):
    work = list(work)
    while work:
        for item in list(work):
            gen, n = item
            for _ in range(n):
                try:
                    next(gen)
                except StopIteration:
                    work.remove(item)
                    break


def _project_in(x_ref, mod_ref, ng_ref, win_ref, widths, out):
    mod = mod_ref[...]
    h = _rms_mod(x_ref[...], ng_ref[...], mod[0:1], mod[1:2]).astype(BF16)
    yield
    o0 = 0
    for w in widths:
        parts = []
        for c in range(0, w, 256):
            cw = min(256, w - c)
            parts.append(jnp.dot(h, win_ref[:, o0 + c:o0 + c + cw], preferred_element_type=F32))
            yield
        out.append(parts[0] if len(parts) == 1 else jnp.concatenate(parts, axis=1))
        o0 += w


def _mixer_seq(proj, x_ref, mod_ref, cw_ref, cb_ref, wg_ref, bgate_ref,
               lam_ref, wg2_ref, bg_ref, gg_ref, wout_ref, o_ref,
               tail_ref, hst_ref, gst_ref, s_idx, *, d_rg, kd, vd):
    ts = x_ref.shape[0]
    dk = kd // GLA_HEADS
    dv = vd // GLA_HEADS
    rg_x, rg_gate, q, k, v, g, f_low = proj

    cat = jnp.concatenate([tail_ref[...], rg_x], axis=0)
    cw = cw_ref[...]
    u = cb_ref[...] + cw[CONV_W - 1:CONV_W] * rg_x
    for tap in range(CONV_W - 1):
        sh = CONV_W - 1 - tap
        u = u + cw[tap:tap + 1] * pltpu.roll(cat, sh, axis=0)[SUBLANES:SUBLANES + ts]
    tail_ref[...] = rg_x[ts - SUBLANES:ts]
    yield

    gates = jnp.dot(u.astype(BF16), wg_ref[...], preferred_element_type=F32) + bgate_ref[...]
    yield
    lam = lam_ref[...]
    sp = jnp.maximum(-lam, 0.0) + jnp.log1p(jnp.exp(-jnp.abs(lam)))
    first_row = (lax.broadcasted_iota(jnp.int32, (ts, LANES), 0) == 0) & (s_idx == 0)
    rg_cols = []
    for c0 in range(0, d_rg, LANES):
        cs = slice(c0, c0 + LANES)
        r = _sigmoid(gates[:, cs])
        i_gate = _sigmoid(gates[:, d_rg + c0:d_rg + c0 + LANES])
        log_a = -RG_C * r * sp[:, cs]
        a = jnp.exp(log_a)
        mult = jnp.sqrt(-jnp.tanh(log_a) * (a * a + 1.0))
        mult = jnp.where(first_row, 1.0, mult)
        b_term = mult * i_gate * u[:, cs]
        h_rg = _linear_scan(a, b_term, hst_ref[0:1, cs])
        hst_ref[:, cs] = jnp.broadcast_to(h_rg[ts - 1:ts], (SUBLANES, LANES))
        gate = rg_gate[:, cs]
        gelu = 0.5 * gate * (1.0 + jnp.tanh(0.7978845608028654 * (gate + 0.044715 * gate ** 3)))
        rg_cols.append(h_rg * gelu)
        yield
    rg_out = jnp.concatenate(rg_cols, axis=1)

    fz = jnp.dot(f_low.astype(BF16), wg2_ref[...], preferred_element_type=F32) + bg_ref[...]
    log_f = (jnp.minimum(fz, 0.0) - jnp.log1p(jnp.exp(-jnp.abs(fz)))) * (1.0 / GLA_TAU)
    bcum = log_f
    d = 1
    while d < GLA_CHUNK:
        bcum = bcum + _shift_rows(bcum, d, 0.0, seg=GLA_CHUNK)
        d *= 2

    yield
    scale = dk ** -0.5
    nh = GLA_HEADS
    c_len = GLA_CHUNK
    r_i = lax.broadcasted_iota(jnp.int32, (nh * c_len, kd), 0)
    c_i = lax.broadcasted_iota(jnp.int32, (nh * c_len, kd), 1)
    head_mask = _idiv(r_i, c_len) == _idiv(c_i, dk)
    r_j = lax.broadcasted_iota(jnp.int32, (nh * c_len, c_len), 0)
    c_j = lax.broadcasted_iota(jnp.int32, (nh * c_len, c_len), 1)
    causal = c_j <= _imod(r_j, c_len)
    r_s = lax.broadcasted_iota(jnp.int32, (vd, kd), 0)
    c_s = lax.broadcasted_iota(jnp.int32, (vd, kd), 1)
    bd_mask = _idiv(r_s, dv) == _idiv(c_s, dk)

    state = gst_ref[...]
    o_chunks = []
    for ci in range(ts // c_len):
        lo = ci * c_len
        bc = bcum[lo:lo + c_len]
        qc = q[lo:lo + c_len] * scale
        kc = k[lo:lo + c_len]
        vc = v[lo:lo + c_len].astype(BF16)
        b_ref_row = bc[c_len // 2 - 1:c_len // 2]
        b_last = bc[c_len - 1:c_len]
        q_loc = (qc * jnp.exp(bc - b_ref_row)).astype(BF16)
        k_loc = (kc * jnp.exp(b_ref_row - bc)).astype(BF16)
        q_stack = jnp.where(head_mask, jnp.concatenate([q_loc] * nh, axis=0), jnp.zeros((), BF16))
        scores = lax.dot_general(q_stack, k_loc, (((1,), (1,)), ((), ())),
                                 preferred_element_type=F32)
        p = jnp.where(causal, scores, 0.0).astype(BF16)
        oi = jnp.dot(p, vc, preferred_element_type=F32)
        o_intra = jnp.concatenate(
            [oi[hh * c_len:(hh + 1) * c_len, hh * dv:(hh + 1) * dv] for hh in range(nh)], axis=1)
        q_b = (qc * jnp.exp(bc)).astype(BF16)
        o_inter = lax.dot_general(q_b, state.astype(BF16), (((1,), (1,)), ((), ())),
                                  preferred_element_type=F32)
        o_chunks.append(o_intra + o_inter)
        k_end = (kc * jnp.exp(b_last - bc)).astype(BF16)
        upd = lax.dot_general(vc, k_end, (((0,), (0,)), ((), ())),
                              preferred_element_type=F32)
        state = jnp.exp(b_last) * state + jnp.where(bd_mask, upd, 0.0)
        yield
    gst_ref[...] = state
    o = jnp.concatenate(o_chunks, axis=0)

    gg = gg_ref[...]
    heads = []
    for hh in range(nh):
        hs = slice(hh * dv, (hh + 1) * dv)
        oh = o[:, hs]
        ms = jnp.mean(oh * oh, axis=-1, keepdims=True)
        gh = g[:, hs]
        heads.append((oh * lax.rsqrt(ms + EPS) * gg) * (gh * _sigmoid(gh)))
        yield
    gla_out = jnp.concatenate(heads, axis=1)

    mix_in = jnp.concatenate([rg_out, gla_out], axis=1).astype(BF16)
    mix = jnp.dot(mix_in, wout_ref[...], preferred_element_type=F32)
    o_ref[...] = x_ref[...] + mod_ref[2:3, :] * mix


def _const_spec(shape):
    nd = len(shape)
    return pl.BlockSpec(shape, lambda *_: (0,) * nd, pipeline_mode=pl.Buffered(1))


def _layer_spec(stacked, layer):
    rest = stacked.shape[1:]
    return pl.BlockSpec((None,) + rest, lambda *_: (layer,) + (0,) * len(rest),
                        pipeline_mode=pl.Buffered(1))


def _mixer(x, mod6, layer, consts):
    bsz, s, d = x.shape
    d_rg = consts[2].shape[2]
    kd = consts[8].shape[2]
    vd = consts[10].shape[1] - d_rg
    ts = MIX_TS
    nseq = MIX_NSEQ if bsz % MIX_NSEQ == 0 else 1
    assert s % ts == 0 and ts % GLA_CHUNK == 0
    kern = functools.partial(_mixer_kernel, d_rg=d_rg, kd=kd, vd=vd)
    return pl.pallas_call(
        kern,
        grid=(bsz // nseq, s // ts),
        in_specs=[
            pl.BlockSpec((nseq, ts, d), lambda b, i: (b, i, 0)),
            pl.BlockSpec((nseq, 6, d), lambda b, i: (b, 0, 0)),
        ] + [_layer_spec(a, layer) for a in consts],
        out_specs=pl.BlockSpec((nseq, ts, d), lambda b, i: (b, i, 0)),
        out_shape=jax.ShapeDtypeStruct(x.shape, F32),
        scratch_shapes=[
            pltpu.VMEM((nseq, SUBLANES, d_rg), F32),
            pltpu.VMEM((nseq, SUBLANES, d_rg), F32),
            pltpu.VMEM((nseq, vd, kd), F32),
        ],
        compiler_params=pltpu.CompilerParams(
            dimension_semantics=("arbitrary", "arbitrary"),
            vmem_limit_bytes=VMEM_LIMIT),
        name="mixer",
    )(x, mod6, *consts)


def _swiglu_acc(h, w1_ref, w3_ref, w2_ref, acc):
    f = w1_ref.shape[-1]
    for j in range(f // FFN_FC):
        sl = slice(j * FFN_FC, (j + 1) * FFN_FC)
        a = jnp.dot(h, w1_ref[:, sl], preferred_element_type=F32)
        b = jnp.dot(h, w3_ref[:, sl], preferred_element_type=F32)
        p = (a * _sigmoid(a) * b).astype(BF16)
        acc = acc + jnp.dot(p, w2_ref[sl, :], preferred_element_type=F32)
    return acc


def _ffn_kernel(x_ref, mod_ref, ng_ref, w1_ref, w3_ref, w2_ref, *rest):
    n_cast = (len(rest) - 1) // 2
    cast_in, o_ref, cast_out = rest[:n_cast], rest[n_cast], rest[n_cast + 1:]
    x = x_ref[...]
    mod = mod_ref[...]
    h = _rms_mod(x, ng_ref[...], mod[3:4], mod[4:5]).astype(BF16)
    acc = _swiglu_acc(h, w1_ref, w3_ref, w2_ref, jnp.zeros(x.shape, F32))
    o_ref[...] = x + mod[5:6] * acc
    for src, dst in zip(cast_in, cast_out):
        dst[...] = src[...].astype(BF16)


def _ffn(x, mod6, ng, layer, w1, w3, w2, j, cast_src, cast_layer):
    bsz, s, d = x.shape
    f = w1.shape[2]
    tm = FFN_TM
    assert s % tm == 0 and f % FFN_FC == 0
    nblk = s // tm
    steps = bsz * nblk
    slabs = []
    for w in cast_src:
        n_layers, n_exp, rows, cols = w.shape
        assert (n_exp * rows) % steps == 0
        slabs.append(w.reshape(n_layers, steps, n_exp * rows // steps, cols))
    outs = pl.pallas_call(
        _ffn_kernel,
        grid=(bsz, nblk),
        in_specs=[
            pl.BlockSpec((None, tm, d), lambda b, i: (b, i, 0)),
            pl.BlockSpec((None, 6, d), lambda b, i: (b, 0, 0)),
            _layer_spec(ng, layer), _layer_spec(w1, j), _layer_spec(w3, j), _layer_spec(w2, j),
        ] + [pl.BlockSpec((None, None) + w.shape[2:], lambda b, i: (cast_layer, b * nblk + i, 0, 0))
             for w in slabs],
        out_specs=[pl.BlockSpec((None, tm, d), lambda b, i: (b, i, 0))]
        + [pl.BlockSpec((None,) + w.shape[2:], lambda b, i: (b * nblk + i, 0, 0)) for w in slabs],
        out_shape=[jax.ShapeDtypeStruct(x.shape, F32)]
        + [jax.ShapeDtypeStruct(w.shape[1:], BF16) for w in slabs],
        compiler_params=pltpu.CompilerParams(
            dimension_semantics=("arbitrary", "arbitrary"),
            vmem_limit_bytes=FFN_VMEM_LIMIT),
        name="ffn_dense",
    )(x, mod6, ng, w1, w3, w2, *slabs)
    cast = [o.reshape(w.shape[1:]) for o, w in zip(outs[1:], cast_src)]
    return outs[0], cast


def _top2(logits, n_exp):
    lane = lax.broadcasted_iota(jnp.int32, logits.shape, 1).astype(F32)
    neg = jnp.float32(-jnp.inf)
    lg = jnp.where(lane < n_exp, logits, neg)
    m1 = jnp.max(lg, axis=-1, keepdims=True)
    i1 = jnp.min(jnp.where(lg == m1, lane, float(LANES)), axis=-1, keepdims=True)
    lg2 = jnp.where(lane == i1, neg, lg)
    m2 = jnp.max(lg2, axis=-1, keepdims=True)
    i2 = jnp.min(jnp.where(lg2 == m2, lane, float(LANES)), axis=-1, keepdims=True)
    ex = jnp.exp(m2 - m1)
    return i1, i2, 1.0 / (1.0 + ex), ex / (1.0 + ex)


def _router_kernel(x_ref, mod_ref, ng_ref, rwhi_ref, rwlo_ref, h_ref, info_ref, meta_ref, cnt_ref,
                   carry_ref, *, n_exp):
    first = (pl.program_id(0) == 0) & (pl.program_id(1) == 0)

    @pl.when(first)
    def _():
        carry_ref[...] = jnp.zeros_like(carry_ref)

    x = x_ref[...]
    mod = mod_ref[...]
    h = _rms_mod(x, ng_ref[...], mod[3:4], mod[4:5])
    h_ref[...] = h
    h_hi = h.astype(BF16)
    h_lo = (h - h_hi.astype(F32)).astype(BF16)
    logits = (jnp.dot(h_hi, rwhi_ref[...], preferred_element_type=F32)
              + (jnp.dot(h_hi, rwlo_ref[...], preferred_element_type=F32)
                 + jnp.dot(h_lo, rwhi_ref[...], preferred_element_type=F32)))
    i1, i2, p1, p2 = _top2(logits, n_exp)
    tr = x.shape[0]
    lane = lax.broadcasted_iota(jnp.int32, (tr, LANES), 1).astype(F32)
    oh1 = lane == i1
    oh2 = lane == i2
    sel = jnp.where(oh1, 1.0, 0.0) + jnp.where(oh2, 1.0, 0.0)
    r_t = lax.broadcasted_iota(jnp.int32, (tr, tr), 0)
    c_t = lax.broadcasted_iota(jnp.int32, (tr, tr), 1)
    tri = jnp.where(r_t > c_t, 1.0, 0.0).astype(BF16)
    excl = jnp.dot(tri, sel.astype(BF16), preferred_element_type=F32)
    base = carry_ref[0:1] + excl
    rank1 = jnp.sum(jnp.where(oh1, base, 0.0), axis=-1, keepdims=True)
    rank2 = jnp.sum(jnp.where(oh2, base, 0.0), axis=-1, keepdims=True)
    info = jnp.where(lane == 0.0, i1, 0.0)
    for col, val in ((1.0, i2), (2.0, rank1), (3.0, rank2), (4.0, p1), (5.0, p2)):
        info = jnp.where(lane == col, val, info)
    info_ref[...] = info
    meta_ref[...] = jnp.transpose(info)[:SUBLANES]
    carry = carry_ref[...] + jnp.sum(sel, axis=0, keepdims=True)
    carry_ref[...] = carry
    cnt_ref[...] = carry


def _router(x, mod6, ng, layer, rw_p, j, n_exp):
    bsz, s, d = x.shape
    tr = ROUTER_TR
    assert s % tr == 0
    nblk = s // tr
    t = bsz * s
    kern = functools.partial(_router_kernel, n_exp=n_exp)
    return pl.pallas_call(
        kern,
        grid=(bsz, nblk),
        in_specs=[
            pl.BlockSpec((None, tr, d), lambda b, i: (b, i, 0)),
            pl.BlockSpec((None, 6, d), lambda b, i: (b, 0, 0)),
            _layer_spec(ng, layer), _layer_spec(rw_p[0], j), _layer_spec(rw_p[1], j),
        ],
        out_specs=[
            pl.BlockSpec((tr, d), lambda b, i: (b * nblk + i, 0)),
            pl.BlockSpec((tr, LANES), lambda b, i: (b * nblk + i, 0)),
            pl.BlockSpec((SUBLANES, tr), lambda b, i: (0, b * nblk + i)),
            pl.BlockSpec((SUBLANES, LANES), lambda b, i: (0, 0)),
        ],
        out_shape=[
            jax.ShapeDtypeStruct((t, d), F32),
            jax.ShapeDtypeStruct((t, LANES), F32),
            jax.ShapeDtypeStruct((SUBLANES, t), F32),
            jax.ShapeDtypeStruct((SUBLANES, LANES), F32),
        ],
        scratch_shapes=[pltpu.VMEM((SUBLANES, LANES), F32)],
        compiler_params=pltpu.CompilerParams(
            dimension_semantics=("arbitrary", "arbitrary"),
            vmem_limit_bytes=VMEM_LIMIT),
        name="moe_router",
    )(x, mod6, ng, *rw_p)


def _row_gather_start(src_hbm, row, dst, dst_row, sem):
    pltpu.make_async_copy(src_hbm.at[pl.ds(row, 1)], dst.at[pl.ds(dst_row, 1)], sem).start()


def _row_gather_wait(src_hbm, dst, sem):
    pltpu.make_async_copy(src_hbm.at[pl.ds(0, dst.shape[0])], dst, sem).wait()


def _expert_kernel(te_ref, nt_ref, cnt_ref, start_ref, d1_ref, d2_ref, h_hbm, w1_ref, w3_ref,
                   w2_ref, o_ref, xbuf, src_ref, sem):
    i = pl.program_id(0)
    n_tiles = nt_ref[0]
    tm = xbuf.shape[1]
    n_tok = d1_ref.shape[0]

    @pl.when(i == 0)
    def _():
        for e in range(cnt_ref.shape[0]):
            lo = start_ref[e] + cnt_ref[e]
            hi = start_ref[e] + _idiv(cnt_ref[e] + (tm - 1), tm) * tm

            def pad_body(p, carry):
                src_ref[p] = 0
                return carry

            lax.fori_loop(lo, hi, pad_body, 0)

        def inv_body(t, carry):
            src_ref[d1_ref[t]] = t
            src_ref[d2_ref[t]] = t
            return carry

        lax.fori_loop(0, n_tok, inv_body, 0, unroll=8)

        def first_body(r, carry):
            _row_gather_start(h_hbm, src_ref[r], xbuf.at[0], r, sem.at[0])
            return carry

        lax.fori_loop(0, tm, first_body, 0, unroll=8)

    @pl.when(i + 1 < n_tiles)
    def _():
        nxt_slot = (i + 1) % 2
        nxt_base = (i + 1) * tm
        for r in range(tm):
            _row_gather_start(h_hbm, src_ref[nxt_base + r], xbuf.at[nxt_slot], r, sem.at[nxt_slot])

    @pl.when(i < n_tiles)
    def _():
        slot = i % 2
        _row_gather_wait(h_hbm, xbuf.at[slot], sem.at[slot])
        xs = xbuf[slot].astype(BF16)
        o_ref[...] = _swiglu_acc(xs, w1_ref, w3_ref, w2_ref, jnp.zeros(o_ref.shape, F32))

    @pl.when(i >= n_tiles)
    def _():
        o_ref[...] = jnp.zeros_like(o_ref)


def _experts(tile_expert, n_tiles, counts, row_start, dest1, dest2, h, w1, w3, w2, nt_max):
    t, d = h.shape
    f = w1.shape[2]
    tm = MOE_TM
    assert f % FFN_FC == 0

    def w_map(i, te, *_):
        return (te[i], 0, 0)

    grid_spec = pltpu.PrefetchScalarGridSpec(
        num_scalar_prefetch=6,
        grid=(nt_max,),
        in_specs=[
            pl.BlockSpec(memory_space=pl.ANY),
            pl.BlockSpec((None, d, f), w_map),
            pl.BlockSpec((None, d, f), w_map),
            pl.BlockSpec((None, f, d), w_map),
        ],
        out_specs=pl.BlockSpec((tm, d), lambda i, *_: (i, 0)),
        scratch_shapes=[
            pltpu.VMEM((2, tm, d), F32),
            pltpu.SMEM((nt_max * tm,), jnp.int32),
            pltpu.SemaphoreType.DMA((2,)),
        ],
    )
    return pl.pallas_call(
        _expert_kernel,
        grid_spec=grid_spec,
        out_shape=jax.ShapeDtypeStruct((nt_max * tm, d), F32),
        compiler_params=pltpu.CompilerParams(
            dimension_semantics=("arbitrary",),
            vmem_limit_bytes=VMEM_LIMIT),
        name="moe_experts",
    )(tile_expert, n_tiles, counts, row_start, dest1, dest2, h, w1, w3, w2)


def _combine_kernel(d1_ref, d2_ref, x_ref, mod_ref, info_ref, fg_ref, y_hbm, o_ref, ybuf, sem,
                    *, final_norm):
    i = pl.program_id(0)
    n = pl.num_programs(0)
    tc = x_ref.shape[0]

    def issue(tile, slot):
        base = tile * tc
        for r in range(tc):
            _row_gather_start(y_hbm, d1_ref[base + r], ybuf.at[slot, 0], r, sem.at[slot])
            _row_gather_start(y_hbm, d2_ref[base + r], ybuf.at[slot, 1], r, sem.at[slot])

    @pl.when(i == 0)
    def _():
        issue(0, 0)

    @pl.when(i + 1 < n)
    def _():
        issue(i + 1, (i + 1) % 2)

    slot = i % 2
    _row_gather_wait(y_hbm, ybuf.at[slot, 0], sem.at[slot])
    _row_gather_wait(y_hbm, ybuf.at[slot, 1], sem.at[slot])
    info = info_ref[...]
    y = info[:, 4:5] * ybuf[slot, 0] + info[:, 5:6] * ybuf[slot, 1]
    out = x_ref[...] + mod_ref[5:6, :] * y
    if final_norm:
        ms = jnp.mean(out * out, axis=-1, keepdims=True)
        out = out * lax.rsqrt(ms + EPS) * fg_ref[...]
    o_ref[...] = out


def _combine(dest1, dest2, x, mod6, info, final_g, y, final_norm):
    bsz, s, d = x.shape
    tc = ROUTER_TR
    nblk = s // tc
    kern = functools.partial(_combine_kernel, final_norm=final_norm)
    grid_spec = pltpu.PrefetchScalarGridSpec(
        num_scalar_prefetch=2,
        grid=(bsz * nblk,),
        in_specs=[
            pl.BlockSpec((None, tc, d), lambda i, d1, d2: (i // nblk, i % nblk, 0)),
            pl.BlockSpec((None, 6, d), lambda i, d1, d2: (i // nblk, 0, 0)),
            pl.BlockSpec((tc, LANES), lambda i, d1, d2: (i, 0)),
            pl.BlockSpec(final_g.shape, lambda i, d1, d2: (0, 0)),
            pl.BlockSpec(memory_space=pl.ANY),
        ],
        out_specs=pl.BlockSpec((None, tc, d), lambda i, d1, d2: (i // nblk, i % nblk, 0)),
        scratch_shapes=[
            pltpu.VMEM((2, 2, tc, d), F32),
            pltpu.SemaphoreType.DMA((2,)),
        ],
    )
    return pl.pallas_call(
        kern,
        grid_spec=grid_spec,
        out_shape=jax.ShapeDtypeStruct(x.shape, F32),
        compiler_params=pltpu.CompilerParams(
            dimension_semantics=("arbitrary",),
            vmem_limit_bytes=VMEM_LIMIT),
        name="moe_combine",
    )(dest1, dest2, x, mod6, info, final_g, y)


def _moe(x, mod6, ng, layer, rw_p, w1, w3, w2, j, final_g, final_norm):
    bsz, s, d = x.shape
    n_exp = w1.shape[0]
    t = bsz * s
    tm = MOE_TM
    nt_max = (TOP_K * t) // tm + n_exp
    h, info, meta, cnt = _router(x, mod6, ng, layer, rw_p, j, n_exp)

    counts = cnt[0, :n_exp].astype(jnp.int32)
    tiles_per = (counts + tm - 1) // tm
    tile_end = jnp.cumsum(tiles_per)
    row_start = (tile_end - tiles_per) * tm
    n_tiles = tile_end[-1]
    meta = meta.astype(jnp.int32)
    dest1 = row_start[meta[0]] + meta[2]
    dest2 = row_start[meta[1]] + meta[3]
    tile_ids = jnp.arange(nt_max, dtype=jnp.int32)
    live_ids = jnp.minimum(tile_ids, n_tiles - 1)
    tile_expert = jnp.sum((live_ids[:, None] >= tile_end[None, :]).astype(jnp.int32), axis=1)
    tile_expert = jnp.minimum(tile_expert, n_exp - 1)

    y = _experts(tile_expert, n_tiles[None].astype(jnp.int32), counts, row_start, dest1, dest2,
                 h, w1, w3, w2, nt_max)
    return _combine(dest1, dest2, x, mod6, info, final_g, y, final_norm)


def _final_norm_kernel(x_ref, g_ref, o_ref):
    x = x_ref[...]
    ms = jnp.mean(x * x, axis=-1, keepdims=True)
    o_ref[...] = x * lax.rsqrt(ms + EPS) * g_ref[...]


def _final_norm(x, g):
    bsz, s, d = x.shape
    tm = FFN_TM
    assert s % tm == 0
    return pl.pallas_call(
        _final_norm_kernel,
        grid=(bsz, s // tm),
        in_specs=[pl.BlockSpec((None, tm, d), lambda b, i: (b, i, 0)),
                  _const_spec(g.shape)],
        out_specs=pl.BlockSpec((None, tm, d), lambda b, i: (b, i, 0)),
        out_shape=jax.ShapeDtypeStruct(x.shape, F32),
        compiler_params=pltpu.CompilerParams(
            dimension_semantics=("arbitrary", "arbitrary"),
            vmem_limit_bytes=VMEM_LIMIT),
        name="final_norm",
    )(x, g)


def _block_diag(w):
    depth, n, a, b = w.shape
    eye = jnp.eye(n, dtype=w.dtype)
    return jnp.einsum('lnde,nm->lndme', w, eye).reshape(depth, n * a, n * b)


def kernel(x, c, ada_w, ada_b, norm_mix_g, norm_ffn_g, w_in, rg_conv_w, rg_conv_b, rg_wa, rg_ba, rg_wx, rg_bx, rg_lambda, gla_wg2, gla_bg, gla_norm_g, w_out, ffn_w1, ffn_w3, ffn_w2, router_w, moe_w1, moe_w3, moe_w2, final_g):
    bsz, s, d = x.shape
    depth = ada_w.shape[0]
    d_rg = rg_conv_w.shape[2]
    rank = gla_wg2.shape[1]
    n_exp = router_w.shape[2]
    d_main = w_in.shape[2] - rank

    mod = _adaln_mod(c, ada_w, ada_b).reshape(depth, bsz, 6, d)

    assert d_main % LANES == 0
    win_p = jnp.pad(w_in.astype(BF16), ((0, 0), (0, 0), (0, LANES - rank)))
    wg2_p = jnp.pad(gla_wg2, ((0, 0), (0, LANES - rank), (0, 0))).astype(BF16)
    wgate = jnp.concatenate([_block_diag(rg_wa), _block_diag(rg_wx)], axis=2).astype(BF16)
    bgate = jnp.concatenate([rg_ba, rg_bx], axis=1)[:, None, :]
    mixer_consts = [norm_mix_g[:, None, :], win_p, rg_conv_w, rg_conv_b[:, None, :], wgate, bgate,
                    rg_lambda[:, None, :], wg2_p, gla_bg[:, None, :], gla_norm_g[:, None, :],
                    w_out.astype(BF16)]
    ng_ffn = norm_ffn_g[:, None, :]
    ffn_w = [w.astype(BF16) for w in (ffn_w1, ffn_w3, ffn_w2)]
    moe_w_f32 = (moe_w1, moe_w3, moe_w2)
    moe_w = None
    rw_f32 = jnp.pad(router_w, ((0, 0), (0, 0), (0, LANES - n_exp)))
    rw_hi = rw_f32.astype(BF16)
    rw_p = (rw_hi, (rw_f32 - rw_hi.astype(F32)).astype(BF16))

    for layer in range(depth):
        x = _mixer(x, mod[layer], layer, mixer_consts)
        j = layer // 2
        if layer % 2 == 0:
            cast_src = moe_w_f32 if layer + 1 < depth else ()
            x, moe_w = _ffn(x, mod[layer], ng_ffn, layer, *ffn_w, j, cast_src, j)
        else:
            last = layer == depth - 1
            x = _moe(x, mod[layer], ng_ffn, layer, rw_p, *moe_w, j, final_g[None, :],
                     final_norm=last)
            if last:
                return x
    return _final_norm(x, final_g[None, :])
```

```python
import functools

import jax
import jax.numpy as jnp
from jax import lax
from jax.experimental import pallas as pl
from jax.experimental.pallas import tpu as pltpu

F32 = jnp.float32
BF16 = jnp.bfloat16

EPS = 1e-6
RG_C = 8.0
CONV_W = 4
RG_BLOCKS = 8
GLA_HEADS = 4
GLA_TAU = 16.0
GLA_CHUNK = 64
TOP_K = 2

LANES = 128
SUBLANES = 8
VMEM_LIMIT = 56 * 1024 * 1024
FFN_VMEM_LIMIT = 60 * 1024 * 1024

MIX_TS = 256
MIX_NSEQ = 4
FFN_TM = 512
FFN_FC = 256
ROUTER_TR = 512
MOE_TM = 512


def _sigmoid(x):
    return 1.0 / (1.0 + jnp.exp(-x))


def _rms_mod(x, g, shift, scale):
    ms = jnp.mean(x * x, axis=-1, keepdims=True)
    return (x * lax.rsqrt(ms + EPS)) * (g * (1.0 + scale)) + shift


def _log2(n):
    assert n > 0 and n & (n - 1) == 0, n
    return n.bit_length() - 1


def _idiv(x, n):
    return lax.shift_right_logical(x, _log2(n))


def _imod(x, n):
    _log2(n)
    return x & (n - 1)


def _shift_rows(x, d, fill, seg=None):
    rolled = pltpu.roll(x, d, axis=0)
    row = lax.broadcasted_iota(jnp.int32, x.shape, 0)
    if seg is not None:
        row = _imod(row, seg)
    return jnp.where(row >= d, rolled, fill)


def _linear_scan(a, b, h0):
    rows, n = a.shape
    groups = rows // SUBLANES
    a3 = a.reshape(groups, SUBLANES, n)
    b3 = b.reshape(groups, SUBLANES, n)
    sub = lax.broadcasted_iota(jnp.int32, a3.shape, 1)
    d = 1
    while d < SUBLANES:
        keep = sub >= d
        b3 = a3 * jnp.where(keep, pltpu.roll(b3, d, axis=1), 0.0) + b3
        a3 = a3 * jnp.where(keep, pltpu.roll(a3, d, axis=1), 1.0)
        d *= 2
    carry = h0
    out = []
    for gi in range(groups):
        hg = a3[gi] * carry + b3[gi]
        out.append(hg)
        carry = hg[SUBLANES - 1:SUBLANES]
    return jnp.concatenate(out, axis=0)


def _mod_kernel(c_ref, w_ref, b_ref, o_ref):
    kk = pl.program_id(1)
    c = c_ref[...]
    ca = c * _sigmoid(c)
    part = jnp.dot(ca.astype(BF16), w_ref[...].astype(BF16), preferred_element_type=F32)

    @pl.when(kk == 0)
    def _():
        o_ref[...] = part + b_ref[...]

    @pl.when(kk != 0)
    def _():
        o_ref[...] += part


def _adaln_mod(c, ada_w, ada_b):
    depth, d, n = ada_w.shape
    bsz = c.shape[0]
    rows = -(-bsz // SUBLANES) * SUBLANES
    c_pad = jnp.pad(c, ((0, rows - bsz), (0, 0)))
    tk = 256
    assert d % tk == 0
    out = pl.pallas_call(
        _mod_kernel,
        grid=(depth, d // tk),
        in_specs=[
            pl.BlockSpec((rows, tk), lambda l, k: (0, k)),
            pl.BlockSpec((None, tk, n), lambda l, k: (l, k, 0)),
            pl.BlockSpec((None, 1, n), lambda l, k: (l, 0, 0)),
        ],
        out_specs=pl.BlockSpec((None, rows, n), lambda l, k: (l, 0, 0)),
        out_shape=jax.ShapeDtypeStruct((depth, rows, n), F32),
        compiler_params=pltpu.CompilerParams(
            dimension_semantics=("arbitrary", "arbitrary"),
            vmem_limit_bytes=VMEM_LIMIT),
        name="adaln_mod",
    )(c_pad, ada_w, ada_b.reshape(depth, 1, n))
    return out[:, :bsz, :]


def _mixer_kernel(x_ref, mod_ref, *rest, **dims):
    consts, (o_ref, tail_ref, hst_ref, gst_ref) = rest[:-4], rest[-4:]
    s_idx = pl.program_id(1)

    @pl.when(s_idx == 0)
    def _():
        tail_ref[...] = jnp.zeros_like(tail_ref)
        hst_ref[...] = jnp.zeros_like(hst_ref)
        gst_ref[...] = jnp.zeros_like(gst_ref)

    ng_ref, win_ref = consts[0], consts[1]
    nseq = x_ref.shape[0]
    widths = (dims["d_rg"], dims["d_rg"], dims["kd"], dims["kd"], dims["vd"], dims["vd"], LANES)
    proj = [[] for _ in range(nseq)]
    for k in range(nseq + 1):
        work = []
        if k < nseq:
            work.append((_project_in(x_ref.at[k], mod_ref.at[k], ng_ref, win_ref, widths, proj[k]), 1))
        if k >= 1:
            j = k - 1
            work.append((_mixer_seq(proj[j], x_ref.at[j], mod_ref.at[j], *consts[2:], o_ref.at[j],
                                    tail_ref.at[j], hst_ref.at[j], gst_ref.at[j], s_idx, **dims), 1))
        _alternate(work)


def _alternate(work):
    work = list(work)
    while work:
        for item in list(work):
            gen, n = item
            for _ in range(n):
                try:
                    next(gen)
                except StopIteration:
                    work.remove(item)
                    break


def _project_in(x_ref, mod_ref, ng_ref, win_ref, widths, out):
    mod = mod_ref[...]
    h = _rms_mod(x_ref[...], ng_ref[...], mod[0:1], mod[1:2]).astype(BF16)
    yield
    o0 = 0
    for w in widths:
        parts = []
        for c in range(0, w, 256):
            cw = min(256, w - c)
            parts.append(jnp.dot(h, win_ref[:, o0 + c:o0 + c + cw], preferred_element_type=F32))
            yield
        out.append(parts[0] if len(parts) == 1 else jnp.concatenate(parts, axis=1))
        o0 += w


def _mixer_seq(proj, x_ref, mod_ref, cw_ref, cb_ref, wg_ref, bgate_ref,
               lam_ref, wg2_ref, bg_ref, gg_ref, wout_ref, o_ref,
               tail_ref, hst_ref, gst_ref, s_idx, *, d_rg, kd, vd):
    ts = x_ref.shape[0]
    dk = kd // GLA_HEADS
    dv = vd // GLA_HEADS
    rg_x, rg_gate, q, k, v, g, f_low = proj

    cat = jnp.concatenate([tail_ref[...], rg_x], axis=0)
    cw = cw_ref[...]
    u = cb_ref[...] + cw[CONV_W - 1:CONV_W] * rg_x
    for tap in range(CONV_W - 1):
        sh = CONV_W - 1 - tap
        u = u + cw[tap:tap + 1] * pltpu.roll(cat, sh, axis=0)[SUBLANES:SUBLANES + ts]
    tail_ref[...] = rg_x[ts - SUBLANES:ts]
    yield

    gates = jnp.dot(u.astype(BF16), wg_ref[...], preferred_element_type=F32) + bgate_ref[...]
    yield
    lam = lam_ref[...]
    sp = jnp.maximum(-lam, 0.0) + jnp.log1p(jnp.exp(-jnp.abs(lam)))
    first_row = (lax.broadcasted_iota(jnp.int32, (ts, LANES), 0) == 0) & (s_idx == 0)
    rg_cols = []
    for c0 in range(0, d_rg, LANES):
        cs = slice(c0, c0 + LANES)
        r = _sigmoid(gates[:, cs])
        i_gate = _sigmoid(gates[:, d_rg + c0:d_rg + c0 + LANES])
        log_a = -RG_C * r * sp[:, cs]
        a = jnp.exp(log_a)
        mult = jnp.sqrt(-jnp.tanh(log_a) * (a * a + 1.0))
        mult = jnp.where(first_row, 1.0, mult)
        b_term = mult * i_gate * u[:, cs]
        h_rg = _linear_scan(a, b_term, hst_ref[0:1, cs])
        hst_ref[:, cs] = jnp.broadcast_to(h_rg[ts - 1:ts], (SUBLANES, LANES))
        gate = rg_gate[:, cs]
        gelu = 0.5 * gate * (1.0 + jnp.tanh(0.7978845608028654 * (gate + 0.044715 * gate ** 3)))
        rg_cols.append(h_rg * gelu)
        yield
    rg_out = jnp.concatenate(rg_cols, axis=1)

    fz = jnp.dot(f_low.astype(BF16), wg2_ref[...], preferred_element_type=F32) + bg_ref[...]
    log_f = (jnp.minimum(fz, 0.0) - jnp.log1p(jnp.exp(-jnp.abs(fz)))) * (1.0 / GLA_TAU)
    bcum = log_f
    d = 1
    while d < GLA_CHUNK:
        bcum = bcum + _shift_rows(bcum, d, 0.0, seg=GLA_CHUNK)
        d *= 2

    yield
    scale = dk ** -0.5
    nh = GLA_HEADS
    c_len = GLA_CHUNK
    r_i = lax.broadcasted_iota(jnp.int32, (nh * c_len, kd), 0)
    c_i = lax.broadcasted_iota(jnp.int32, (nh * c_len, kd), 1)
    head_mask = _idiv(r_i, c_len) == _idiv(c_i, dk)
    r_j = lax.broadcasted_iota(jnp.int32, (nh * c_len, c_len), 0)
    c_j = lax.broadcasted_iota(jnp.int32, (nh * c_len, c_len), 1)
    causal = c_j <= _imod(r_j, c_len)
    r_s = lax.broadcasted_iota(jnp.int32, (vd, kd), 0)
    c_s = lax.broadcasted_iota(jnp.int32, (vd, kd), 1)
    bd_mask = _idiv(r_s, dv) == _idiv(c_s, dk)

    state = gst_ref[...]
    o_chunks = []
    for ci in range(ts // c_len):
        lo = ci * c_len
        bc = bcum[lo:lo + c_len]
        qc = q[lo:lo + c_len] * scale
        kc = k[lo:lo + c_len]
        vc = v[lo:lo + c_len].astype(BF16)
        b_ref_row = bc[c_len // 2 - 1:c_len // 2]
        b_last = bc[c_len - 1:c_len]
        q_loc = (qc * jnp.exp(bc - b_ref_row)).astype(BF16)
        k_loc = (kc * jnp.exp(b_ref_row - bc)).astype(BF16)
        q_stack = jnp.where(head_mask, jnp.concatenate([q_loc] * nh, axis=0), jnp.zeros((), BF16))
        scores = lax.dot_general(q_stack, k_loc, (((1,), (1,)), ((), ())),
                                 preferred_element_type=F32)
        p = jnp.where(causal, scores, 0.0).astype(BF16)
        oi = jnp.dot(p, vc, preferred_element_type=F32)
        o_intra = jnp.concatenate(
            [oi[hh * c_len:(hh + 1) * c_len, hh * dv:(hh + 1) * dv] for hh in range(nh)], axis=1)
        q_b = (qc * jnp.exp(bc)).astype(BF16)
        o_inter = lax.dot_general(q_b, state.astype(BF16), (((1,), (1,)), ((), ())),
                                  preferred_element_type=F32)
        o_chunks.append(o_intra + o_inter)
        k_end = (kc * jnp.exp(b_last - bc)).astype(BF16)
        upd = lax.dot_general(vc, k_end, (((0,), (0,)), ((), ())),
                              preferred_element_type=F32)
        state = jnp.exp(b_last) * state + jnp.where(bd_mask, upd, 0.0)
        yield
    gst_ref[...] = state
    o = jnp.concatenate(o_chunks, axis=0)

    gg = gg_ref[...]
    heads = []
    for hh in range(nh):
        hs = slice(hh * dv, (hh + 1) * dv)
        oh = o[:, hs]
        ms = jnp.mean(oh * oh, axis=-1, keepdims=True)
        gh = g[:, hs]
        heads.append((oh * lax.rsqrt(ms + EPS) * gg) * (gh * _sigmoid(gh)))
        yield
    gla_out = jnp.concatenate(heads, axis=1)

    mix_in = jnp.concatenate([rg_out, gla_out], axis=1).astype(BF16)
    mix = jnp.dot(mix_in, wout_ref[...], preferred_element_type=F32)
    o_ref[...] = x_ref[...] + mod_ref[2:3, :] * mix


def _const_spec(shape):
    nd = len(shape)
    return pl.BlockSpec(shape, lambda *_: (0,) * nd, pipeline_mode=pl.Buffered(1))


def _layer_spec(stacked, layer):
    rest = stacked.shape[1:]
    return pl.BlockSpec((None,) + rest, lambda *_: (layer,) + (0,) * len(rest),
                        pipeline_mode=pl.Buffered(1))


def _mixer(x, mod6, layer, consts):
    bsz, s, d = x.shape
    d_rg = consts[2].shape[2]
    kd = consts[8].shape[2]
    vd = consts[10].shape[1] - d_rg
    ts = MIX_TS
    nseq = MIX_NSEQ if bsz % MIX_NSEQ == 0 else 1
    assert s % ts == 0 and ts % GLA_CHUNK == 0
    kern = functools.partial(_mixer_kernel, d_rg=d_rg, kd=kd, vd=vd)
    return pl.pallas_call(
        kern,
        grid=(bsz // nseq, s // ts),
        in_specs=[
            pl.BlockSpec((nseq, ts, d), lambda b, i: (b, i, 0)),
            pl.BlockSpec((nseq, 6, d), lambda b, i: (b, 0, 0)),
        ] + [_layer_spec(a, layer) for a in consts],
        out_specs=pl.BlockSpec((nseq, ts, d), lambda b, i: (b, i, 0)),
        out_shape=jax.ShapeDtypeStruct(x.shape, F32),
        scratch_shapes=[
            pltpu.VMEM((nseq, SUBLANES, d_rg), F32),
            pltpu.VMEM((nseq, SUBLANES, d_rg), F32),
            pltpu.VMEM((nseq, vd, kd), F32),
        ],
        compiler_params=pltpu.CompilerParams(
            dimension_semantics=("arbitrary", "arbitrary"),
            vmem_limit_bytes=VMEM_LIMIT),
        name="mixer",
    )(x, mod6, *consts)


def _swiglu_acc(h, w1_ref, w3_ref, w2_ref, acc):
    f = w1_ref.shape[-1]
    for j in range(f // FFN_FC):
        sl = slice(j * FFN_FC, (j + 1) * FFN_FC)
        a = jnp.dot(h, w1_ref[:, sl], preferred_element_type=F32)
        b = jnp.dot(h, w3_ref[:, sl], preferred_element_type=F32)
        p = (a * _sigmoid(a) * b).astype(BF16)
        acc = acc + jnp.dot(p, w2_ref[sl, :], preferred_element_type=F32)
    return acc


def _ffn_kernel(x_ref, mod_ref, ng_ref, w1_ref, w3_ref, w2_ref, *rest):
    n_cast = (len(rest) - 1) // 2
    cast_in, o_ref, cast_out = rest[:n_cast], rest[n_cast], rest[n_cast + 1:]
    x = x_ref[...]
    mod = mod_ref[...]
    h = _rms_mod(x, ng_ref[...], mod[3:4], mod[4:5]).astype(BF16)
    acc = _swiglu_acc(h, w1_ref, w3_ref, w2_ref, jnp.zeros(x.shape, F32))
    o_ref[...] = x + mod[5:6] * acc
    for src, dst in zip(cast_in, cast_out):
        dst[...] = src[...].astype(BF16)


def _ffn(x, mod6, ng, layer, w1, w3, w2, j, cast_src, cast_layer):
    bsz, s, d = x.shape
    f = w1.shape[2]
    tm = FFN_TM
    assert s % tm == 0 and f % FFN_FC == 0
    nblk = s // tm
    steps = bsz * nblk
    slabs = []
    for w in cast_src:
        n_layers, n_exp, rows, cols = w.shape
        assert (n_exp * rows) % steps == 0
        slabs.append(w.reshape(n_layers, steps, n_exp * rows // steps, cols))
    outs = pl.pallas_call(
        _ffn_kernel,
        grid=(bsz, nblk),
        in_specs=[
            pl.BlockSpec((None, tm, d), lambda b, i: (b, i, 0)),
            pl.BlockSpec((None, 6, d), lambda b, i: (b, 0, 0)),
            _layer_spec(ng, layer), _layer_spec(w1, j), _layer_spec(w3, j), _layer_spec(w2, j),
        ] + [pl.BlockSpec((None, None) + w.shape[2:], lambda b, i: (cast_layer, b * nblk + i, 0, 0))
             for w in slabs],
        out_specs=[pl.BlockSpec((None, tm, d), lambda b, i: (b, i, 0))]
        + [pl.BlockSpec((None,) + w.shape[2:], lambda b, i: (b * nblk + i, 0, 0)) for w in slabs],
        out_shape=[jax.ShapeDtypeStruct(x.shape, F32)]
        + [jax.ShapeDtypeStruct(w.shape[1:], BF16) for w in slabs],
        compiler_params=pltpu.CompilerParams(
            dimension_semantics=("arbitrary", "arbitrary"),
            vmem_limit_bytes=FFN_VMEM_LIMIT),
        name="ffn_dense",
    )(x, mod6, ng, w1, w3, w2, *slabs)
    cast = [o.reshape(w.shape[1:]) for o, w in zip(outs[1:], cast_src)]
    return outs[0], cast


def _top2(logits, n_exp):
    lane = lax.broadcasted_iota(jnp.int32, logits.shape, 1).astype(F32)
    neg = jnp.float32(-jnp.inf)
    lg = jnp.where(lane < n_exp, logits, neg)
    m1 = jnp.max(lg, axis=-1, keepdims=True)
    i1 = jnp.min(jnp.where(lg == m1, lane, float(LANES)), axis=-1, keepdims=True)
    lg2 = jnp.where(lane == i1, neg, lg)
    m2 = jnp.max(lg2, axis=-1, keepdims=True)
    i2 = jnp.min(jnp.where(lg2 == m2, lane, float(LANES)), axis=-1, keepdims=True)
    ex = jnp.exp(m2 - m1)
    return i1, i2, 1.0 / (1.0 + ex), ex / (1.0 + ex)


def _router_kernel(x_ref, mod_ref, ng_ref, rwhi_ref, rwlo_ref, h_ref, info_ref, meta_ref, cnt_ref,
                   carry_ref, *, n_exp):
    first = (pl.program_id(0) == 0) & (pl.program_id(1) == 0)

    @pl.when(first)
    def _():
        carry_ref[...] = jnp.zeros_like(carry_ref)

    x = x_ref[...]
    mod = mod_ref[...]
    h = _rms_mod(x, ng_ref[...], mod[3:4], mod[4:5])
    h_ref[...] = h
    h_hi = h.astype(BF16)
    h_lo = (h - h_hi.astype(F32)).astype(BF16)
    logits = (jnp.dot(h_hi, rwhi_ref[...], preferred_element_type=F32)
              + (jnp.dot(h_hi, rwlo_ref[...], preferred_element_type=F32)
                 + jnp.dot(h_lo, rwhi_ref[...], preferred_element_type=F32)))
    i1, i2, p1, p2 = _top2(logits, n_exp)
    tr = x.shape[0]
    lane = lax.broadcasted_iota(jnp.int32, (tr, LANES), 1).astype(F32)
    oh1 = lane == i1
    oh2 = lane == i2
    sel = jnp.where(oh1, 1.0, 0.0) + jnp.where(oh2, 1.0, 0.0)
    r_t = lax.broadcasted_iota(jnp.int32, (tr, tr), 0)
    c_t = lax.broadcasted_iota(jnp.int32, (tr, tr), 1)
    tri = jnp.where(r_t > c_t, 1.0, 0.0).astype(BF16)
    excl = jnp.dot(tri, sel.astype(BF16), preferred_element_type=F32)
    base = carry_ref[0:1] + excl
    rank1 = jnp.sum(jnp.where(oh1, base, 0.0), axis=-1, keepdims=True)
    rank2 = jnp.sum(jnp.where(oh2, base, 0.0), axis=-1, keepdims=True)
    info = jnp.where(lane == 0.0, i1, 0.0)
    for col, val in ((1.0, i2), (2.0, rank1), (3.0, rank2), (4.0, p1), (5.0, p2)):
        info = jnp.where(lane == col, val, info)
    info_ref[...] = info
    meta_ref[...] = jnp.transpose(info)[:SUBLANES]
    carry = carry_ref[...] + jnp.sum(sel, axis=0, keepdims=True)
    carry_ref[...] = carry
    cnt_ref[...] = carry


def _router(x, mod6, ng, layer, rw_p, j, n_exp):
    bsz, s, d = x.shape
    tr = ROUTER_TR
    assert s % tr == 0
    nblk = s // tr
    t = bsz * s
    kern = functools.partial(_router_kernel, n_exp=n_exp)
    return pl.pallas_call(
        kern,
        grid=(bsz, nblk),
        in_specs=[
            pl.BlockSpec((None, tr, d), lambda b, i: (b, i, 0)),
            pl.BlockSpec((None, 6, d), lambda b, i: (b, 0, 0)),
            _layer_spec(ng, layer), _layer_spec(rw_p[0], j), _layer_spec(rw_p[1], j),
        ],
        out_specs=[
            pl.BlockSpec((tr, d), lambda b, i: (b * nblk + i, 0)),
            pl.BlockSpec((tr, LANES), lambda b, i: (b * nblk + i, 0)),
            pl.BlockSpec((SUBLANES, tr), lambda b, i: (0, b * nblk + i)),
            pl.BlockSpec((SUBLANES, LANES), lambda b, i: (0, 0)),
        ],
        out_shape=[
            jax.ShapeDtypeStruct((t, d), F32),
            jax.ShapeDtypeStruct((t, LANES), F32),
            jax.ShapeDtypeStruct((SUBLANES, t), F32),
            jax.ShapeDtypeStruct((SUBLANES, LANES), F32),
        ],
        scratch_shapes=[pltpu.VMEM((SUBLANES, LANES), F32)],
        compiler_params=pltpu.CompilerParams(
            dimension_semantics=("arbitrary", "arbitrary"),
            vmem_limit_bytes=VMEM_LIMIT),
        name="moe_router",
    )(x, mod6, ng, *rw_p)


def _row_gather_start(src_hbm, row, dst, dst_row, sem):
    pltpu.make_async_copy(src_hbm.at[pl.ds(row, 1)], dst.at[pl.ds(dst_row, 1)], sem).start()


def _row_gather_wait(src_hbm, dst, sem):
    pltpu.make_async_copy(src_hbm.at[pl.ds(0, dst.shape[0])], dst, sem).wait()


def _expert_kernel(te_ref, nt_ref, cnt_ref, start_ref, d1_ref, d2_ref, h_hbm, w1_ref, w3_ref,
                   w2_ref, o_ref, xbuf, src_ref, sem):
    i = pl.program_id(0)
    n_tiles = nt_ref[0]
    tm = xbuf.shape[1]
    n_tok = d1_ref.shape[0]

    @pl.when(i == 0)
    def _():
        for e in range(cnt_ref.shape[0]):
            lo = start_ref[e] + cnt_ref[e]
            hi = start_ref[e] + _idiv(cnt_ref[e] + (tm - 1), tm) * tm

            def pad_body(p, carry):
                src_ref[p] = 0
                return carry

            lax.fori_loop(lo, hi, pad_body, 0)

        def inv_body(t, carry):
            src_ref[d1_ref[t]] = t
            src_ref[d2_ref[t]] = t
            return carry

        lax.fori_loop(0, n_tok, inv_body, 0, unroll=8)

        def first_body(r, carry):
            _row_gather_start(h_hbm, src_ref[r], xbuf.at[0], r, sem.at[0])
            return carry

        lax.fori_loop(0, tm, first_body, 0, unroll=8)

    @pl.when(i + 1 < n_tiles)
    def _():
        nxt_slot = (i + 1) % 2
        nxt_base = (i + 1) * tm
        for r in range(tm):
            _row_gather_start(h_hbm, src_ref[nxt_base + r], xbuf.at[nxt_slot], r, sem.at[nxt_slot])

    @pl.when(i < n_tiles)
    def _():
        slot = i % 2
        _row_gather_wait(h_hbm, xbuf.at[slot], sem.at[slot])
        e = te_ref[i]
        live_rows = start_ref[e] + cnt_ref[e] - i * tm
        half = tm // 2

        @pl.when(live_rows > half)
        def _():
            xs = xbuf[slot].astype(BF16)
            o_ref[...] = _swiglu_acc(xs, w1_ref, w3_ref, w2_ref, jnp.zeros(o_ref.shape, F32))

        @pl.when(live_rows <= half)
        def _():
            xs = xbuf[slot, :half].astype(BF16)
            o_ref[:half] = _swiglu_acc(xs, w1_ref, w3_ref, w2_ref,
                                       jnp.zeros((half, o_ref.shape[1]), F32))
            o_ref[half:] = jnp.zeros((tm - half, o_ref.shape[1]), F32)

    @pl.when(i >= n_tiles)
    def _():
        o_ref[...] = jnp.zeros_like(o_ref)


def _experts(tile_expert, n_tiles, counts, row_start, dest1, dest2, h, w1, w3, w2, nt_max):
    t, d = h.shape
    f = w1.shape[2]
    tm = MOE_TM
    assert f % FFN_FC == 0

    def w_map(i, te, *_):
        return (te[i], 0, 0)

    grid_spec = pltpu.PrefetchScalarGridSpec(
        num_scalar_prefetch=6,
        grid=(nt_max,),
        in_specs=[
            pl.BlockSpec(memory_space=pl.ANY),
            pl.BlockSpec((None, d, f), w_map),
            pl.BlockSpec((None, d, f), w_map),
            pl.BlockSpec((None, f, d), w_map),
        ],
        out_specs=pl.BlockSpec((tm, d), lambda i, *_: (i, 0)),
        scratch_shapes=[
            pltpu.VMEM((2, tm, d), F32),
            pltpu.SMEM((nt_max * tm,), jnp.int32),
            pltpu.SemaphoreType.DMA((2,)),
        ],
    )
    return pl.pallas_call(
        _expert_kernel,
        grid_spec=grid_spec,
        out_shape=jax.ShapeDtypeStruct((nt_max * tm, d), F32),
        compiler_params=pltpu.CompilerParams(
            dimension_semantics=("arbitrary",),
            vmem_limit_bytes=VMEM_LIMIT),
        name="moe_experts",
    )(tile_expert, n_tiles, counts, row_start, dest1, dest2, h, w1, w3, w2)


def _combine_kernel(d1_ref, d2_ref, x_ref, mod_ref, info_ref, fg_ref, y_hbm, o_ref, ybuf, sem,
                    *, final_norm):
    i = pl.program_id(0)
    n = pl.num_programs(0)
    tc = x_ref.shape[0]

    def issue(tile, slot):
        base = tile * tc
        for r in range(tc):
            _row_gather_start(y_hbm, d1_ref[base + r], ybuf.at[slot, 0], r, sem.at[slot])
            _row_gather_start(y_hbm, d2_ref[base + r], ybuf.at[slot, 1], r, sem.at[slot])

    @pl.when(i == 0)
    def _():
        issue(0, 0)

    @pl.when(i + 1 < n)
    def _():
        issue(i + 1, (i + 1) % 2)

    slot = i % 2
    _row_gather_wait(y_hbm, ybuf.at[slot, 0], sem.at[slot])
    _row_gather_wait(y_hbm, ybuf.at[slot, 1], sem.at[slot])
    info = info_ref[...]
    y = info[:, 4:5] * ybuf[slot, 0] + info[:, 5:6] * ybuf[slot, 1]
    out = x_ref[...] + mod_ref[5:6, :] * y
    if final_norm:
        ms = jnp.mean(out * out, axis=-1, keepdims=True)
        out = out * lax.rsqrt(ms + EPS) * fg_ref[...]
    o_ref[...] = out


def _combine(dest1, dest2, x, mod6, info, final_g, y, final_norm):
    bsz, s, d = x.shape
    tc = ROUTER_TR
    nblk = s // tc
    kern = functools.partial(_combine_kernel, final_norm=final_norm)
    grid_spec = pltpu.PrefetchScalarGridSpec(
        num_scalar_prefetch=2,
        grid=(bsz * nblk,),
        in_specs=[
            pl.BlockSpec((None, tc, d), lambda i, d1, d2: (i // nblk, i % nblk, 0)),
            pl.BlockSpec((None, 6, d), lambda i, d1, d2: (i // nblk, 0, 0)),
            pl.BlockSpec((tc, LANES), lambda i, d1, d2: (i, 0)),
            pl.BlockSpec(final_g.shape, lambda i, d1, d2: (0, 0)),
            pl.BlockSpec(memory_space=pl.ANY),
        ],
        out_specs=pl.BlockSpec((None, tc, d), lambda i, d1, d2: (i // nblk, i % nblk, 0)),
        scratch_shapes=[
            pltpu.VMEM((2, 2, tc, d), F32),
            pltpu.SemaphoreType.DMA((2,)),
        ],
    )
    return pl.pallas_call(
        kern,
        grid_spec=grid_spec,
        out_shape=jax.ShapeDtypeStruct(x.shape, F32),
        compiler_params=pltpu.CompilerParams(
            dimension_semantics=("arbitrary",),
            vmem_limit_bytes=VMEM_LIMIT),
        name="moe_combine",
    )(dest1, dest2, x, mod6, info, final_g, y)


def _moe(x, mod6, ng, layer, rw_p, w1, w3, w2, j, final_g, final_norm):
    bsz, s, d = x.shape
    n_exp = w1.shape[0]
    t = bsz * s
    tm = MOE_TM
    nt_max = (TOP_K * t) // tm + n_exp
    h, info, meta, cnt = _router(x, mod6, ng, layer, rw_p, j, n_exp)

    counts = cnt[0, :n_exp].astype(jnp.int32)
    tiles_per = (counts + tm - 1) // tm
    tile_end = jnp.cumsum(tiles_per)
    row_start = (tile_end - tiles_per) * tm
    n_tiles = tile_end[-1]
    meta = meta.astype(jnp.int32)
    dest1 = row_start[meta[0]] + meta[2]
    dest2 = row_start[meta[1]] + meta[3]
    tile_ids = jnp.arange(nt_max, dtype=jnp.int32)
    live_ids = jnp.minimum(tile_ids, n_tiles - 1)
    tile_expert = jnp.sum((live_ids[:, None] >= tile_end[None, :]).astype(jnp.int32), axis=1)
    tile_expert = jnp.minimum(tile_expert, n_exp - 1)

    y = _experts(tile_expert, n_tiles[None].astype(jnp.int32), counts, row_start, dest1, dest2,
                 h, w1, w3, w2, nt_max)
    return _combine(dest1, dest2, x, mod6, info, final_g, y, final_norm)


def _final_norm_kernel(x_ref, g_ref, o_ref):
    x = x_ref[...]
    ms = jnp.mean(x * x, axis=-1, keepdims=True)
    o_ref[...] = x * lax.rsqrt(ms + EPS) * g_ref[...]


def _final_norm(x, g):
    bsz, s, d = x.shape
    tm = FFN_TM
    assert s % tm == 0
    return pl.pallas_call(
        _final_norm_kernel,
        grid=(bsz, s // tm),
        in_specs=[pl.BlockSpec((None, tm, d), lambda b, i: (b, i, 0)),
                  _const_spec(g.shape)],
        out_specs=pl.BlockSpec((None, tm, d), lambda b, i: (b, i, 0)),
        out_shape=jax.ShapeDtypeStruct(x.shape, F32),
        compiler_params=pltpu.CompilerParams(
            dimension_semantics=("arbitrary", "arbitrary"),
            vmem_limit_bytes=VMEM_LIMIT),
        name="final_norm",
    )(x, g)


def _block_diag(w):
    depth, n, a, b = w.shape
    eye = jnp.eye(n, dtype=w.dtype)
    return jnp.einsum('lnde,nm->lndme', w, eye).reshape(depth, n * a, n * b)


def kernel(x, c, ada_w, ada_b, norm_mix_g, norm_ffn_g, w_in, rg_conv_w, rg_conv_b, rg_wa, rg_ba, rg_wx, rg_bx, rg_lambda, gla_wg2, gla_bg, gla_norm_g, w_out, ffn_w1, ffn_w3, ffn_w2, router_w, moe_w1, moe_w3, moe_w2, final_g):
    bsz, s, d = x.shape
    depth = ada_w.shape[0]
    d_rg = rg_conv_w.shape[2]
    rank = gla_wg2.shape[1]
    n_exp = router_w.shape[2]
    d_main = w_in.shape[2] - rank

    mod = _adaln_mod(c, ada_w, ada_b).reshape(depth, bsz, 6, d)

    assert d_main % LANES == 0
    win_p = jnp.pad(w_in.astype(BF16), ((0, 0), (0, 0), (0, LANES - rank)))
    wg2_p = jnp.pad(gla_wg2, ((0, 0), (0, LANES - rank), (0, 0))).astype(BF16)
    wgate = jnp.concatenate([_block_diag(rg_wa), _block_diag(rg_wx)], axis=2).astype(BF16)
    bgate = jnp.concatenate([rg_ba, rg_bx], axis=1)[:, None, :]
    mixer_consts = [norm_mix_g[:, None, :], win_p, rg_conv_w, rg_conv_b[:, None, :], wgate, bgate,
                    rg_lambda[:, None, :], wg2_p, gla_bg[:, None, :], gla_norm_g[:, None, :],
                    w_out.astype(BF16)]
    ng_ffn = norm_ffn_g[:, None, :]
    ffn_w = [w.astype(BF16) for w in (ffn_w1, ffn_w3, ffn_w2)]
    moe_w_f32 = (moe_w1, moe_w3, moe_w2)
    moe_w = None
    rw_f32 = jnp.pad(router_w, ((0, 0), (0, 0), (0, LANES - n_exp)))
    rw_hi = rw_f32.astype(BF16)
    rw_p = (rw_hi, (rw_f32 - rw_hi.astype(F32)).astype(BF16))

    for layer in range(depth):
        x = _mixer(x, mod[layer], layer, mixer_consts)
        j = layer // 2
        if layer % 2 == 0:
            cast_src = moe_w_f32 if layer + 1 < depth else ()
            x, moe_w = _ffn(x, mod[layer], ng_ffn, layer, *ffn_w, j, cast_src, j)
        else:
            last = layer == depth - 1
            x = _moe(x, mod[layer], ng_ffn, layer, rw_p, *moe_w, j, final_g[None, :],
                     final_norm=last)
            if last:
                return x
    return _final_norm(x, final_g[None, :])
```

```python
import functools

import jax
import jax.numpy as jnp
from jax import lax
from jax.experimental import pallas as pl
from jax.experimental.pallas import tpu as pltpu

F32 = jnp.float32
BF16 = jnp.bfloat16

EPS = 1e-6
RG_C = 8.0
CONV_W = 4
RG_BLOCKS = 8
GLA_HEADS = 4
GLA_TAU = 16.0
GLA_CHUNK = 64
TOP_K = 2

LANES = 128
SUBLANES = 8
VMEM_LIMIT = 56 * 1024 * 1024
FFN_VMEM_LIMIT = 60 * 1024 * 1024

MIX_TS = 256
MIX_NSEQ = 4
FFN_TM = 512
FFN_FC = 256
ROUTER_TR = 512
MOE_TM = 512


def _sigmoid(x):
    return 1.0 / (1.0 + jnp.exp(-x))


def _rms_mod(x, g, shift, scale):
    ms = jnp.mean(x * x, axis=-1, keepdims=True)
    return (x * lax.rsqrt(ms + EPS)) * (g * (1.0 + scale)) + shift


def _log2(n):
    assert n > 0 and n & (n - 1) == 0, n
    return n.bit_length() - 1


def _idiv(x, n):
    return lax.shift_right_logical(x, _log2(n))


def _imod(x, n):
    _log2(n)
    return x & (n - 1)


def _shift_rows(x, d, fill, seg=None):
    rolled = pltpu.roll(x, d, axis=0)
    row = lax.broadcasted_iota(jnp.int32, x.shape, 0)
    if seg is not None:
        row = _imod(row, seg)
    return jnp.where(row >= d, rolled, fill)


def _linear_scan(a, b, h0):
    rows, n = a.shape
    groups = rows // SUBLANES
    a3 = a.reshape(groups, SUBLANES, n)
    b3 = b.reshape(groups, SUBLANES, n)
    sub = lax.broadcasted_iota(jnp.int32, a3.shape, 1)
    d = 1
    while d < SUBLANES:
        keep = sub >= d
        b3 = a3 * jnp.where(keep, pltpu.roll(b3, d, axis=1), 0.0) + b3
        a3 = a3 * jnp.where(keep, pltpu.roll(a3, d, axis=1), 1.0)
        d *= 2
    carry = h0
    out = []
    for gi in range(groups):
        hg = a3[gi] * carry + b3[gi]
        out.append(hg)
        carry = hg[SUBLANES - 1:SUBLANES]
    return jnp.concatenate(out, axis=0)


def _mod_kernel(c_ref, w_ref, b_ref, o_ref):
    kk = pl.program_id(1)
    c = c_ref[...]
    ca = c * _sigmoid(c)
    part = jnp.dot(ca.astype(BF16), w_ref[...].astype(BF16), preferred_element_type=F32)

    @pl.when(kk == 0)
    def _():
        o_ref[...] = part + b_ref[...]

    @pl.when(kk != 0)
    def _():
        o_ref[...] += part


def _adaln_mod(c, ada_w, ada_b):
    depth, d, n = ada_w.shape
    bsz = c.shape[0]
    rows = -(-bsz // SUBLANES) * SUBLANES
    c_pad = jnp.pad(c, ((0, rows - bsz), (0, 0)))
    tk = 256
    assert d % tk == 0
    out = pl.pallas_call(
        _mod_kernel,
        grid=(depth, d // tk),
        in_specs=[
            pl.BlockSpec((rows, tk), lambda l, k: (0, k)),
            pl.BlockSpec((None, tk, n), lambda l, k: (l, k, 0)),
            pl.BlockSpec((None, 1, n), lambda l, k: (l, 0, 0)),
        ],
        out_specs=pl.BlockSpec((None, rows, n), lambda l, k: (l, 0, 0)),
        out_shape=jax.ShapeDtypeStruct((depth, rows, n), F32),
        compiler_params=pltpu.CompilerParams(
            dimension_semantics=("arbitrary", "arbitrary"),
            vmem_limit_bytes=VMEM_LIMIT),
        name="adaln_mod",
    )(c_pad, ada_w, ada_b.reshape(depth, 1, n))
    return out[:, :bsz, :]


def _mixer_kernel(x_ref, mod_ref, *rest, **dims):
    consts, (o_ref, tail_ref, hst_ref, gst_ref) = rest[:-4], rest[-4:]
    s_idx = pl.program_id(1)

    @pl.when(s_idx == 0)
    def _():
        tail_ref[...] = jnp.zeros_like(tail_ref)
        hst_ref[...] = jnp.zeros_like(hst_ref)
        gst_ref[...] = jnp.zeros_like(gst_ref)

    ng_ref, win_ref = consts[0], consts[1]
    nseq = x_ref.shape[0]
    widths = (dims["d_rg"], dims["d_rg"], dims["kd"], dims["kd"], dims["vd"], dims["vd"], LANES)
    proj = [[] for _ in range(nseq)]
    for k in range(nseq + 1):
        work = []
        if k < nseq:
            work.append((_project_in(x_ref.at[k], mod_ref.at[k], ng_ref, win_ref, widths, proj[k]), 1))
        if k >= 1:
            j = k - 1
            work.append((_mixer_seq(proj[j], x_ref.at[j], mod_ref.at[j], *consts[2:], o_ref.at[j],
                                    tail_ref.at[j], hst_ref.at[j], gst_ref.at[j], s_idx, **dims), 1))
        _alternate(work)


def _alternate(work):
    work = list(work)
    while work:
        for item in list(work):
            gen, n = item
            for _ in range(n):
                try:
                    next(gen)
                except StopIteration:
                    work.remove(item)
                    break


def _project_in(x_ref, mod_ref, ng_ref, win_ref, widths, out):
    mod = mod_ref[...]
    h = _rms_mod(x_ref[...], ng_ref[...], mod[0:1], mod[1:2]).astype(BF16)
    yield
    o0 = 0
    for w in widths:
        parts = []
        for c in range(0, w, 256):
            cw = min(256, w - c)
            parts.append(jnp.dot(h, win_ref[:, o0 + c:o0 + c + cw], preferred_element_type=F32))
            yield
        out.append(parts[0] if len(parts) == 1 else jnp.concatenate(parts, axis=1))
        o0 += w


def _mixer_seq(proj, x_ref, mod_ref, cw_ref, cb_ref, wg_ref, bgate_ref,
               lam_ref, wg2_ref, bg_ref, gg_ref, wout_ref, o_ref,
               tail_ref, hst_ref, gst_ref, s_idx, *, d_rg, kd, vd):
    ts = x_ref.shape[0]
    dk = kd // GLA_HEADS
    dv = vd // GLA_HEADS
    rg_x, rg_gate, q, k, v, g, f_low = proj

    cat = jnp.concatenate([tail_ref[...], rg_x], axis=0)
    cw = cw_ref[...]
    u = cb_ref[...] + cw[CONV_W - 1:CONV_W] * rg_x
    for tap in range(CONV_W - 1):
        sh = CONV_W - 1 - tap
        u = u + cw[tap:tap + 1] * pltpu.roll(cat, sh, axis=0)[SUBLANES:SUBLANES + ts]
    tail_ref[...] = rg_x[ts - SUBLANES:ts]
    yield

    gates = jnp.dot(u.astype(BF16), wg_ref[...], preferred_element_type=F32) + bgate_ref[...]
    yield
    lam = lam_ref[...]
    sp = jnp.maximum(-lam, 0.0) + jnp.log1p(jnp.exp(-jnp.abs(lam)))
    first_row = (lax.broadcasted_iota(jnp.int32, (ts, LANES), 0) == 0) & (s_idx == 0)
    rg_cols = []
    for c0 in range(0, d_rg, LANES):
        cs = slice(c0, c0 + LANES)
        r = _sigmoid(gates[:, cs])
        i_gate = _sigmoid(gates[:, d_rg + c0:d_rg + c0 + LANES])
        log_a = -RG_C * r * sp[:, cs]
        a = jnp.exp(log_a)
        mult = jnp.sqrt(-jnp.tanh(log_a) * (a * a + 1.0))
        mult = jnp.where(first_row, 1.0, mult)
        b_term = mult * i_gate * u[:, cs]
        h_rg = _linear_scan(a, b_term, hst_ref[0:1, cs])
        hst_ref[:, cs] = jnp.broadcast_to(h_rg[ts - 1:ts], (SUBLANES, LANES))
        gate = rg_gate[:, cs]
        gelu = 0.5 * gate * (1.0 + jnp.tanh(0.7978845608028654 * (gate + 0.044715 * gate ** 3)))
        rg_cols.append(h_rg * gelu)
        yield
    rg_out = jnp.concatenate(rg_cols, axis=1)

    fz = jnp.dot(f_low.astype(BF16), wg2_ref[...], preferred_element_type=F32) + bg_ref[...]
    log_f = (jnp.minimum(fz, 0.0) - jnp.log1p(jnp.exp(-jnp.abs(fz)))) * (1.0 / GLA_TAU)
    bcum = log_f
    d = 1
    while d < GLA_CHUNK:
        bcum = bcum + _shift_rows(bcum, d, 0.0, seg=GLA_CHUNK)
        d *= 2

    yield
    scale = dk ** -0.5
    nh = GLA_HEADS
    c_len = GLA_CHUNK
    r_i = lax.broadcasted_iota(jnp.int32, (nh * c_len, kd), 0)
    c_i = lax.broadcasted_iota(jnp.int32, (nh * c_len, kd), 1)
    head_mask = _idiv(r_i, c_len) == _idiv(c_i, dk)
    r_j = lax.broadcasted_iota(jnp.int32, (nh * c_len, c_len), 0)
    c_j = lax.broadcasted_iota(jnp.int32, (nh * c_len, c_len), 1)
    causal = c_j <= _imod(r_j, c_len)
    r_s = lax.broadcasted_iota(jnp.int32, (vd, kd), 0)
    c_s = lax.broadcasted_iota(jnp.int32, (vd, kd), 1)
    bd_mask = _idiv(r_s, dv) == _idiv(c_s, dk)

    state = gst_ref[...]
    o_chunks = []
    for ci in range(ts // c_len):
        lo = ci * c_len
        bc = bcum[lo:lo + c_len]
        qc = q[lo:lo + c_len] * scale
        kc = k[lo:lo + c_len]
        vc = v[lo:lo + c_len].astype(BF16)
        b_ref_row = bc[c_len // 2 - 1:c_len // 2]
        b_last = bc[c_len - 1:c_len]
        q_loc = (qc * jnp.exp(bc - b_ref_row)).astype(BF16)
        k_loc = (kc * jnp.exp(b_ref_row - bc)).astype(BF16)
        q_stack = jnp.where(head_mask, jnp.concatenate([q_loc] * nh, axis=0), jnp.zeros((), BF16))
        scores = lax.dot_general(q_stack, k_loc, (((1,), (1,)), ((), ())),
                                 preferred_element_type=F32)
        p = jnp.where(causal, scores, 0.0).astype(BF16)
        oi = jnp.dot(p, vc, preferred_element_type=F32)
        o_intra = jnp.concatenate(
            [oi[hh * c_len:(hh + 1) * c_len, hh * dv:(hh + 1) * dv] for hh in range(nh)], axis=1)
        q_b = (qc * jnp.exp(bc)).astype(BF16)
        o_inter = lax.dot_general(q_b, state.astype(BF16), (((1,), (1,)), ((), ())),
                                  preferred_element_type=F32)
        o_chunks.append(o_intra + o_inter)
        k_end = (kc * jnp.exp(b_last - bc)).astype(BF16)
        upd = lax.dot_general(vc, k_end, (((0,), (0,)), ((), ())),
                              preferred_element_type=F32)
        state = jnp.exp(b_last) * state + jnp.where(bd_mask, upd, 0.0)
        yield
    gst_ref[...] = state
    o = jnp.concatenate(o_chunks, axis=0)

    gg = gg_ref[...]
    heads = []
    for hh in range(nh):
        hs = slice(hh * dv, (hh + 1) * dv)
        oh = o[:, hs]
        ms = jnp.mean(oh * oh, axis=-1, keepdims=True)
        gh = g[:, hs]
        heads.append((oh * lax.rsqrt(ms + EPS) * gg) * (gh * _sigmoid(gh)))
        yield
    gla_out = jnp.concatenate(heads, axis=1)

    mix_in = jnp.concatenate([rg_out, gla_out], axis=1).astype(BF16)
    mix = jnp.dot(mix_in, wout_ref[...], preferred_element_type=F32)
    o_ref[...] = x_ref[...] + mod_ref[2:3, :] * mix


def _const_spec(shape):
    nd = len(shape)
    return pl.BlockSpec(shape, lambda *_: (0,) * nd, pipeline_mode=pl.Buffered(1))


def _layer_spec(stacked, layer):
    rest = stacked.shape[1:]
    return pl.BlockSpec((None,) + rest, lambda *_: (layer,) + (0,) * len(rest),
                        pipeline_mode=pl.Buffered(1))


def _mixer(x, mod6, layer, consts):
    bsz, s, d = x.shape
    d_rg = consts[2].shape[2]
    kd = consts[8].shape[2]
    vd = consts[10].shape[1] - d_rg
    ts = MIX_TS
    nseq = MIX_NSEQ if bsz % MIX_NSEQ == 0 else 1
    assert s % ts == 0 and ts % GLA_CHUNK == 0
    kern = functools.partial(_mixer_kernel, d_rg=d_rg, kd=kd, vd=vd)
    return pl.pallas_call(
        kern,
        grid=(bsz // nseq, s // ts),
        in_specs=[
            pl.BlockSpec((nseq, ts, d), lambda b, i: (b, i, 0)),
            pl.BlockSpec((nseq, 6, d), lambda b, i: (b, 0, 0)),
        ] + [_layer_spec(a, layer) for a in consts],
        out_specs=pl.BlockSpec((nseq, ts, d), lambda b, i: (b, i, 0)),
        out_shape=jax.ShapeDtypeStruct(x.shape, F32),
        scratch_shapes=[
            pltpu.VMEM((nseq, SUBLANES, d_rg), F32),
            pltpu.VMEM((nseq, SUBLANES, d_rg), F32),
            pltpu.VMEM((nseq, vd, kd), F32),
        ],
        compiler_params=pltpu.CompilerParams(
            dimension_semantics=("arbitrary", "arbitrary"),
            vmem_limit_bytes=VMEM_LIMIT),
        name="mixer",
    )(x, mod6, *consts)


def _swiglu_acc(h, w1_ref, w3_ref, w2_ref, acc):
    f = w1_ref.shape[-1]
    for j in range(f // FFN_FC):
        sl = slice(j * FFN_FC, (j + 1) * FFN_FC)
        a = jnp.dot(h, w1_ref[:, sl], preferred_element_type=F32)
        b = jnp.dot(h, w3_ref[:, sl], preferred_element_type=F32)
        p = (a * _sigmoid(a) * b).astype(BF16)
        acc = acc + jnp.dot(p, w2_ref[sl, :], preferred_element_type=F32)
    return acc


def _ffn_kernel(x_ref, mod_ref, ng_ref, w1_ref, w3_ref, w2_ref, *rest):
    n_cast = (len(rest) - 1) // 2
    cast_in, o_ref, cast_out = rest[:n_cast], rest[n_cast], rest[n_cast + 1:]
    x = x_ref[...]
    mod = mod_ref[...]
    h = _rms_mod(x, ng_ref[...], mod[3:4], mod[4:5]).astype(BF16)
    acc = _swiglu_acc(h, w1_ref, w3_ref, w2_ref, jnp.zeros(x.shape, F32))
    o_ref[...] = x + mod[5:6] * acc
    for src, dst in zip(cast_in, cast_out):
        dst[...] = src[...].astype(BF16)


def _ffn(x, mod6, ng, layer, w1, w3, w2, j, cast_src, cast_layer):
    bsz, s, d = x.shape
    f = w1.shape[2]
    tm = FFN_TM
    assert s % tm == 0 and f % FFN_FC == 0
    nblk = s // tm
    steps = bsz * nblk
    slabs = []
    for w in cast_src:
        n_layers, n_exp, rows, cols = w.shape
        assert (n_exp * rows) % steps == 0
        slabs.append(w.reshape(n_layers, steps, n_exp * rows // steps, cols))
    outs = pl.pallas_call(
        _ffn_kernel,
        grid=(bsz, nblk),
        in_specs=[
            pl.BlockSpec((None, tm, d), lambda b, i: (b, i, 0)),
            pl.BlockSpec((None, 6, d), lambda b, i: (b, 0, 0)),
            _layer_spec(ng, layer), _layer_spec(w1, j), _layer_spec(w3, j), _layer_spec(w2, j),
        ] + [pl.BlockSpec((None, None) + w.shape[2:], lambda b, i: (cast_layer, b * nblk + i, 0, 0))
             for w in slabs],
        out_specs=[pl.BlockSpec((None, tm, d), lambda b, i: (b, i, 0))]
        + [pl.BlockSpec((None,) + w.shape[2:], lambda b, i: (b * nblk + i, 0, 0)) for w in slabs],
        out_shape=[jax.ShapeDtypeStruct(x.shape, F32)]
        + [jax.ShapeDtypeStruct(w.shape[1:], BF16) for w in slabs],
        compiler_params=pltpu.CompilerParams(
            dimension_semantics=("arbitrary", "arbitrary"),
            vmem_limit_bytes=FFN_VMEM_LIMIT),
        name="ffn_dense",
    )(x, mod6, ng, w1, w3, w2, *slabs)
    cast = [o.reshape(w.shape[1:]) for o, w in zip(outs[1:], cast_src)]
    return outs[0], cast


def _top2(logits, n_exp):
    lane = lax.broadcasted_iota(jnp.int32, logits.shape, 1).astype(F32)
    neg = jnp.float32(-jnp.inf)
    lg = jnp.where(lane < n_exp, logits, neg)
    m1 = jnp.max(lg, axis=-1, keepdims=True)
    i1 = jnp.min(jnp.where(lg == m1, lane, float(LANES)), axis=-1, keepdims=True)
    lg2 = jnp.where(lane == i1, neg, lg)
    m2 = jnp.max(lg2, axis=-1, keepdims=True)
    i2 = jnp.min(jnp.where(lg2 == m2, lane, float(LANES)), axis=-1, keepdims=True)
    ex = jnp.exp(m2 - m1)
    return i1, i2, 1.0 / (1.0 + ex), ex / (1.0 + ex)


def _router_kernel(x_ref, mod_ref, ng_ref, rwhi_ref, rwlo_ref, h_ref, info_ref, meta_ref, cnt_ref,
                   carry_ref, *, n_exp):
    first = (pl.program_id(0) == 0) & (pl.program_id(1) == 0)

    @pl.when(first)
    def _():
        carry_ref[...] = jnp.zeros_like(carry_ref)

    x = x_ref[...]
    mod = mod_ref[...]
    h = _rms_mod(x, ng_ref[...], mod[3:4], mod[4:5])
    h_ref[...] = h
    h_hi = h.astype(BF16)
    h_lo = (h - h_hi.astype(F32)).astype(BF16)
    logits = (jnp.dot(h_hi, rwhi_ref[...], preferred_element_type=F32)
              + (jnp.dot(h_hi, rwlo_ref[...], preferred_element_type=F32)
                 + jnp.dot(h_lo, rwhi_ref[...], preferred_element_type=F32)))
    i1, i2, p1, p2 = _top2(logits, n_exp)
    tr = x.shape[0]
    lane = lax.broadcasted_iota(jnp.int32, (tr, LANES), 1).astype(F32)
    oh1 = lane == i1
    oh2 = lane == i2
    sel = jnp.where(oh1, 1.0, 0.0) + jnp.where(oh2, 1.0, 0.0)
    r_t = lax.broadcasted_iota(jnp.int32, (tr, tr), 0)
    c_t = lax.broadcasted_iota(jnp.int32, (tr, tr), 1)
    tri = jnp.where(r_t > c_t, 1.0, 0.0).astype(BF16)
    excl = jnp.dot(tri, sel.astype(BF16), preferred_element_type=F32)
    base = carry_ref[0:1] + excl
    rank1 = jnp.sum(jnp.where(oh1, base, 0.0), axis=-1, keepdims=True)
    rank2 = jnp.sum(jnp.where(oh2, base, 0.0), axis=-1, keepdims=True)
    info = jnp.where(lane == 0.0, i1, 0.0)
    for col, val in ((1.0, i2), (2.0, rank1), (3.0, rank2), (4.0, p1), (5.0, p2)):
        info = jnp.where(lane == col, val, info)
    info_ref[...] = info
    meta_ref[...] = jnp.transpose(info)[:SUBLANES]
    carry = carry_ref[...] + jnp.sum(sel, axis=0, keepdims=True)
    carry_ref[...] = carry
    cnt_ref[...] = carry


def _router(x, mod6, ng, layer, rw_p, j, n_exp):
    bsz, s, d = x.shape
    tr = ROUTER_TR
    assert s % tr == 0
    nblk = s // tr
    t = bsz * s
    kern = functools.partial(_router_kernel, n_exp=n_exp)
    return pl.pallas_call(
        kern,
        grid=(bsz, nblk),
        in_specs=[
            pl.BlockSpec((None, tr, d), lambda b, i: (b, i, 0)),
            pl.BlockSpec((None, 6, d), lambda b, i: (b, 0, 0)),
            _layer_spec(ng, layer), _layer_spec(rw_p[0], j), _layer_spec(rw_p[1], j),
        ],
        out_specs=[
            pl.BlockSpec((tr, d), lambda b, i: (b * nblk + i, 0)),
            pl.BlockSpec((tr, LANES), lambda b, i: (b * nblk + i, 0)),
            pl.BlockSpec((SUBLANES, tr), lambda b, i: (0, b * nblk + i)),
            pl.BlockSpec((SUBLANES, LANES), lambda b, i: (0, 0)),
        ],
        out_shape=[
            jax.ShapeDtypeStruct((t, d), F32),
            jax.ShapeDtypeStruct((t, LANES), F32),
            jax.ShapeDtypeStruct((SUBLANES, t), F32),
            jax.ShapeDtypeStruct((SUBLANES, LANES), F32),
        ],
        scratch_shapes=[pltpu.VMEM((SUBLANES, LANES), F32)],
        compiler_params=pltpu.CompilerParams(
            dimension_semantics=("arbitrary", "arbitrary"),
            vmem_limit_bytes=VMEM_LIMIT),
        name="moe_router",
    )(x, mod6, ng, *rw_p)


def _row_gather_start(src_hbm, row, dst, dst_row, sem, priority=0):
    pltpu.make_async_copy(src_hbm.at[pl.ds(row, 1)], dst.at[pl.ds(dst_row, 1)], sem).start(
        priority=priority)


def _row_gather_wait(src_hbm, dst, sem):
    pltpu.make_async_copy(src_hbm.at[pl.ds(0, dst.shape[0])], dst, sem).wait()


def _expert_kernel(te_ref, nt_ref, cnt_ref, start_ref, d1_ref, d2_ref, h_hbm, w1_ref, w3_ref,
                   w2_ref, o_ref, xbuf, src_ref, sem):
    i = pl.program_id(0)
    n_tiles = nt_ref[0]
    tm = xbuf.shape[1]
    n_tok = d1_ref.shape[0]

    @pl.when(i == 0)
    def _():
        for e in range(cnt_ref.shape[0]):
            lo = start_ref[e] + cnt_ref[e]
            hi = start_ref[e] + _idiv(cnt_ref[e] + (tm - 1), tm) * tm

            def pad_body(p, carry):
                src_ref[p] = 0
                return carry

            lax.fori_loop(lo, hi, pad_body, 0)

        def inv_body(t, carry):
            src_ref[d1_ref[t]] = t
            src_ref[d2_ref[t]] = t
            return carry

        lax.fori_loop(0, n_tok, inv_body, 0, unroll=8)

        def first_body(r, carry):
            _row_gather_start(h_hbm, src_ref[r], xbuf.at[0], r, sem.at[0])
            return carry

        lax.fori_loop(0, tm, first_body, 0, unroll=8)

    @pl.when(i + 1 < n_tiles)
    def _():
        nxt_slot = (i + 1) % 2
        nxt_base = (i + 1) * tm
        for r in range(tm):
            _row_gather_start(h_hbm, src_ref[nxt_base + r], xbuf.at[nxt_slot], r, sem.at[nxt_slot])

    @pl.when(i < n_tiles)
    def _():
        slot = i % 2
        _row_gather_wait(h_hbm, xbuf.at[slot], sem.at[slot])
        e = te_ref[i]
        live_rows = start_ref[e] + cnt_ref[e] - i * tm
        half = tm // 2

        @pl.when(live_rows > half)
        def _():
            xs = xbuf[slot].astype(BF16)
            o_ref[...] = _swiglu_acc(xs, w1_ref, w3_ref, w2_ref, jnp.zeros(o_ref.shape, F32))

        @pl.when(live_rows <= half)
        def _():
            xs = xbuf[slot, :half].astype(BF16)
            o_ref[:half] = _swiglu_acc(xs, w1_ref, w3_ref, w2_ref,
                                       jnp.zeros((half, o_ref.shape[1]), F32))
            o_ref[half:] = jnp.zeros((tm - half, o_ref.shape[1]), F32)

    @pl.when(i >= n_tiles)
    def _():
        o_ref[...] = jnp.zeros_like(o_ref)


def _experts(tile_expert, n_tiles, counts, row_start, dest1, dest2, h, w1, w3, w2, nt_max):
    t, d = h.shape
    f = w1.shape[2]
    tm = MOE_TM
    assert f % FFN_FC == 0

    def w_map(i, te, *_):
        return (te[i], 0, 0)

    grid_spec = pltpu.PrefetchScalarGridSpec(
        num_scalar_prefetch=6,
        grid=(nt_max,),
        in_specs=[
            pl.BlockSpec(memory_space=pl.ANY),
            pl.BlockSpec((None, d, f), w_map),
            pl.BlockSpec((None, d, f), w_map),
            pl.BlockSpec((None, f, d), w_map),
        ],
        out_specs=pl.BlockSpec((tm, d), lambda i, *_: (i, 0)),
        scratch_shapes=[
            pltpu.VMEM((2, tm, d), F32),
            pltpu.SMEM((nt_max * tm,), jnp.int32),
            pltpu.SemaphoreType.DMA((2,)),
        ],
    )
    return pl.pallas_call(
        _expert_kernel,
        grid_spec=grid_spec,
        out_shape=jax.ShapeDtypeStruct((nt_max * tm, d), F32),
        compiler_params=pltpu.CompilerParams(
            dimension_semantics=("arbitrary",),
            vmem_limit_bytes=VMEM_LIMIT),
        name="moe_experts",
    )(tile_expert, n_tiles, counts, row_start, dest1, dest2, h, w1, w3, w2)


def _combine_kernel(d1_ref, d2_ref, x_ref, mod_ref, info_ref, fg_ref, y_hbm, o_ref, ybuf, sem,
                    *, final_norm):
    i = pl.program_id(0)
    n = pl.num_programs(0)
    tc = x_ref.shape[0]

    def issue(tile, slot):
        base = tile * tc
        for r in range(tc):
            _row_gather_start(y_hbm, d1_ref[base + r], ybuf.at[slot, 0], r, sem.at[slot], priority=0)
            _row_gather_start(y_hbm, d2_ref[base + r], ybuf.at[slot, 1], r, sem.at[slot], priority=1)

    @pl.when(i == 0)
    def _():
        issue(0, 0)

    @pl.when(i + 1 < n)
    def _():
        issue(i + 1, (i + 1) % 2)

    slot = i % 2
    _row_gather_wait(y_hbm, ybuf.at[slot, 0], sem.at[slot])
    _row_gather_wait(y_hbm, ybuf.at[slot, 1], sem.at[slot])
    info = info_ref[...]
    y = info[:, 4:5] * ybuf[slot, 0] + info[:, 5:6] * ybuf[slot, 1]
    out = x_ref[...] + mod_ref[5:6, :] * y
    if final_norm:
        ms = jnp.mean(out * out, axis=-1, keepdims=True)
        out = out * lax.rsqrt(ms + EPS) * fg_ref[...]
    o_ref[...] = out


def _combine(dest1, dest2, x, mod6, info, final_g, y, final_norm):
    bsz, s, d = x.shape
    tc = ROUTER_TR
    nblk = s // tc
    kern = functools.partial(_combine_kernel, final_norm=final_norm)
    grid_spec = pltpu.PrefetchScalarGridSpec(
        num_scalar_prefetch=2,
        grid=(bsz * nblk,),
        in_specs=[
            pl.BlockSpec((None, tc, d), lambda i, d1, d2: (i // nblk, i % nblk, 0)),
            pl.BlockSpec((None, 6, d), lambda i, d1, d2: (i // nblk, 0, 0)),
            pl.BlockSpec((tc, LANES), lambda i, d1, d2: (i, 0)),
            pl.BlockSpec(final_g.shape, lambda i, d1, d2: (0, 0)),
            pl.BlockSpec(memory_space=pl.ANY),
        ],
        out_specs=pl.BlockSpec((None, tc, d), lambda i, d1, d2: (i // nblk, i % nblk, 0)),
        scratch_shapes=[
            pltpu.VMEM((2, 2, tc, d), F32),
            pltpu.SemaphoreType.DMA((2,)),
        ],
    )
    return pl.pallas_call(
        kern,
        grid_spec=grid_spec,
        out_shape=jax.ShapeDtypeStruct(x.shape, F32),
        compiler_params=pltpu.CompilerParams(
            dimension_semantics=("arbitrary",),
            vmem_limit_bytes=VMEM_LIMIT),
        name="moe_combine",
    )(dest1, dest2, x, mod6, info, final_g, y)


def _moe(x, mod6, ng, layer, rw_p, w1, w3, w2, j, final_g, final_norm):
    bsz, s, d = x.shape
    n_exp = w1.shape[0]
    t = bsz * s
    tm = MOE_TM
    nt_max = (TOP_K * t) // tm + n_exp
    h, info, meta, cnt = _router(x, mod6, ng, layer, rw_p, j, n_exp)

    counts = cnt[0, :n_exp].astype(jnp.int32)
    tiles_per = (counts + tm - 1) // tm
    tile_end = jnp.cumsum(tiles_per)
    row_start = (tile_end - tiles_per) * tm
    n_tiles = tile_end[-1]
    meta = meta.astype(jnp.int32)
    dest1 = row_start[meta[0]] + meta[2]
    dest2 = row_start[meta[1]] + meta[3]
    tile_ids = jnp.arange(nt_max, dtype=jnp.int32)
    live_ids = jnp.minimum(tile_ids, n_tiles - 1)
    tile_expert = jnp.sum((live_ids[:, None] >= tile_end[None, :]).astype(jnp.int32), axis=1)
    tile_expert = jnp.minimum(tile_expert, n_exp - 1)

    y = _experts(tile_expert, n_tiles[None].astype(jnp.int32), counts, row_start, dest1, dest2,
                 h, w1, w3, w2, nt_max)
    return _combine(dest1, dest2, x, mod6, info, final_g, y, final_norm)


def _final_norm_kernel(x_ref, g_ref, o_ref):
    x = x_ref[...]
    ms = jnp.mean(x * x, axis=-1, keepdims=True)
    o_ref[...] = x * lax.rsqrt(ms + EPS) * g_ref[...]


def _final_norm(x, g):
    bsz, s, d = x.shape
    tm = FFN_TM
    assert s % tm == 0
    return pl.pallas_call(
        _final_norm_kernel,
        grid=(bsz, s // tm),
        in_specs=[pl.BlockSpec((None, tm, d), lambda b, i: (b, i, 0)),
                  _const_spec(g.shape)],
        out_specs=pl.BlockSpec((None, tm, d), lambda b, i: (b, i, 0)),
        out_shape=jax.ShapeDtypeStruct(x.shape, F32),
        compiler_params=pltpu.CompilerParams(
            dimension_semantics=("arbitrary", "arbitrary"),
            vmem_limit_bytes=VMEM_LIMIT),
        name="final_norm",
    )(x, g)


def _block_diag(w):
    depth, n, a, b = w.shape
    eye = jnp.eye(n, dtype=w.dtype)
    return jnp.einsum('lnde,nm->lndme', w, eye).reshape(depth, n * a, n * b)


def kernel(x, c, ada_w, ada_b, norm_mix_g, norm_ffn_g, w_in, rg_conv_w, rg_conv_b, rg_wa, rg_ba, rg_wx, rg_bx, rg_lambda, gla_wg2, gla_bg, gla_norm_g, w_out, ffn_w1, ffn_w3, ffn_w2, router_w, moe_w1, moe_w3, moe_w2, final_g):
    bsz, s, d = x.shape
    depth = ada_w.shape[0]
    d_rg = rg_conv_w.shape[2]
    rank = gla_wg2.shape[1]
    n_exp = router_w.shape[2]
    d_main = w_in.shape[2] - rank

    mod = _adaln_mod(c, ada_w, ada_b).reshape(depth, bsz, 6, d)

    assert d_main % LANES == 0
    win_p = jnp.pad(w_in.astype(BF16), ((0, 0), (0, 0), (0, LANES - rank)))
    wg2_p = jnp.pad(gla_wg2, ((0, 0), (0, LANES - rank), (0, 0))).astype(BF16)
    wgate = jnp.concatenate([_block_diag(rg_wa), _block_diag(rg_wx)], axis=2).astype(BF16)
    bgate = jnp.concatenate([rg_ba, rg_bx], axis=1)[:, None, :]
    mixer_consts = [norm_mix_g[:, None, :], win_p, rg_conv_w, rg_conv_b[:, None, :], wgate, bgate,
                    rg_lambda[:, None, :], wg2_p, gla_bg[:, None, :], gla_norm_g[:, None, :],
                    w_out.astype(BF16)]
    ng_ffn = norm_ffn_g[:, None, :]
    ffn_w = [w.astype(BF16) for w in (ffn_w1, ffn_w3, ffn_w2)]
    moe_w_f32 = (moe_w1, moe_w3, moe_w2)
    moe_w = None
    rw_f32 = jnp.pad(router_w, ((0, 0), (0, 0), (0, LANES - n_exp)))
    rw_hi = rw_f32.astype(BF16)
    rw_p = (rw_hi, (rw_f32 - rw_hi.astype(F32)).astype(BF16))

    for layer in range(depth):
        x = _mixer(x, mod[layer], layer, mixer_consts)
        j = layer // 2
        if layer % 2 == 0:
            cast_src = moe_w_f32 if layer + 1 < depth else ()
            x, moe_w = _ffn(x, mod[layer], ng_ffn, layer, *ffn_w, j, cast_src, j)
        else:
            last = layer == depth - 1
            x = _moe(x, mod[layer], ng_ffn, layer, rw_p, *moe_w, j, final_g[None, :],
                     final_norm=last)
            if last:
                return x
    return _final_norm(x, final_g[None, :])
```
